```python
import math
import jax, jax.numpy as jnp
from jax import lax
import numpy as np

D_MODEL = 1024
BATCH = 4
SEQ = 8192
DEPTH = 2
DEC_BATCH = 1
DEC_SEQ = 16384
PAST_LEN = 128

GRID_W = 64
HEAD_DIM = 128
A_GROUPS = ((128, 1), (512, 4), (2048, 16))
A_N_GROUPS = 3
A_HEADS_PER_GROUP = 4
A_HEADS = A_N_GROUPS * A_HEADS_PER_GROUP
A_QKV_WIDTH = A_HEADS * HEAD_DIM
A_WIDTH = A_HEADS_PER_GROUP * HEAD_DIM
ALIBI_MAX_BIAS = 8.0
B_HEADS = 8
B_KV_HEADS = 2
B_WIDTH = B_HEADS * HEAD_DIM
B_KV_WIDTH = B_KV_HEADS * HEAD_DIM
Q_BLOCK = 128
ROPE_THETA = 10000.0
GATE_WIDTH = 2 * D_MODEL
IN_WIDTHS = (A_QKV_WIDTH, A_QKV_WIDTH, A_QKV_WIDTH, A_WIDTH,
             B_WIDTH, B_KV_WIDTH, B_KV_WIDTH, B_WIDTH, D_MODEL, D_MODEL)
IN_WIDTH = 3 * A_QKV_WIDTH + A_WIDTH + 2 * B_WIDTH + 2 * B_KV_WIDTH + GATE_WIDTH
RMS_EPS = 1e-6
LN_EPS = 1e-5
MASK_VALUE = -1e30
ALPHA = (2 * DEPTH) ** 0.25
BETA = (8 * DEPTH) ** -0.25

kernel_name = "hybrid_dilated_gqa_encoder"


def layer_norm(x, g, b):
    xf = x.astype(jnp.float32)
    mu = jnp.mean(xf, axis=-1, keepdims=True)
    var = jnp.mean(jnp.square(xf - mu), axis=-1, keepdims=True)
    y = (xf - mu) * lax.rsqrt(var + LN_EPS) * g.astype(jnp.float32) + b.astype(jnp.float32)
    return y.astype(x.dtype)


def rms_norm(x, g):
    xf = x.astype(jnp.float32)
    y = xf * lax.rsqrt(jnp.mean(jnp.square(xf), axis=-1, keepdims=True) + RMS_EPS) * g.astype(jnp.float32)
    return y.astype(x.dtype)


def alibi_slopes():
    return jnp.asarray(2.0 ** (-ALIBI_MAX_BIAS * np.arange(1, A_HEADS + 1) / A_HEADS), dtype=jnp.float32)


def banded_attention(q, k, v, slopes, dil, n_side):
    N, L, H, Dh = q.shape
    blk = n_side
    nb = -(-L // blk)
    Lp = nb * blk
    qb = jnp.pad(q, ((0, 0), (0, Lp - L), (0, 0), (0, 0))).reshape(N, nb, blk, H, Dh)
    pad_kv = ((0, 0), (blk, Lp - L + blk), (0, 0), (0, 0))

    def neighbourhood(t):
        tb = jnp.pad(t, pad_kv).reshape(N, nb + 2, blk, H, Dh)
        return jnp.concatenate([tb[:, :-2], tb[:, 1:-1], tb[:, 2:]], axis=2)

    kb = neighbourhood(k)
    vb = neighbourhood(v)
    rel = jnp.arange(3 * blk)[None, :] - blk - jnp.arange(blk)[:, None]
    key_pos = jnp.arange(nb)[:, None] * blk - blk + jnp.arange(3 * blk)[None, :]
    valid = (jnp.abs(rel) <= n_side)[None] & ((key_pos >= 0) & (key_pos < L))[:, None, :]
    alibi = -slopes[:, None, None] * (dil * jnp.abs(rel)).astype(jnp.float32)[None]
    scale = 1.0 / math.sqrt(Dh)
    s = jnp.einsum('nbqhd,nbkhd->nbhqk', qb, kb).astype(jnp.float32) * scale + alibi[None, None]
    s = jnp.where(valid[None, :, None], s, MASK_VALUE)
    m = jnp.max(s, axis=-1, keepdims=True)
    p = jnp.exp(s - m)
    l = jnp.sum(p, axis=-1, keepdims=True)
    o = jnp.einsum('nbhqk,nbkhd->nbqhd', (p / l).astype(v.dtype), vb)
    lse = (m + jnp.log(l))[..., 0]
    o = o.reshape(N, Lp, H, Dh)[:, :L]
    lse = lse.transpose(0, 1, 3, 2).reshape(N, Lp, H)[:, :L]
    return o, lse


def dilated_group(q, k, v, slopes, dil, n_side):
    B, S, H, Dh = q.shape
    L = S // dil

    def to_classes(t):
        return t.reshape(B, L, dil, H, Dh).transpose(0, 2, 1, 3, 4).reshape(B * dil, L, H, Dh)

    o, lse = banded_attention(to_classes(q), to_classes(k), to_classes(v), slopes, dil, n_side)
    o = o.reshape(B, dil, L, H, Dh).transpose(0, 2, 1, 3, 4).reshape(B, S, H, Dh)
    lse = lse.reshape(B, dil, L, H).transpose(0, 2, 1, 3).reshape(B, S, H)
    return o, lse


def dilated_mixer(q, k, v, slopes):
    B, S, _ = q.shape
    shp = (B, S, A_N_GROUPS, A_HEADS_PER_GROUP, HEAD_DIM)
    q, k, v = q.reshape(shp), k.reshape(shp), v.reshape(shp)
    slopes = slopes.reshape(A_N_GROUPS, A_HEADS_PER_GROUP)
    outs, lses = [], []
    for g, (window, dil) in enumerate(A_GROUPS):
        n_side = window // (2 * dil)
        o, lse = dilated_group(q[:, :, g], k[:, :, g], v[:, :, g], slopes[g], dil, n_side)
        outs.append(o)
        lses.append(lse)
    o = jnp.stack(outs)
    w = jax.nn.softmax(jnp.stack(lses), axis=0)
    out = jnp.einsum('gbsh,gbshd->bshd', w.astype(o.dtype), o)
    return out.reshape(B, S, A_WIDTH)


def axial_rope_tables(S):
    rows = S // GRID_W
    row_pos = jnp.repeat(jnp.arange(rows), GRID_W).astype(jnp.float32)
    col_pos = jnp.tile(jnp.arange(GRID_W), rows).astype(jnp.float32)
    axis_dim = HEAD_DIM // 2
    inv_freq = ROPE_THETA ** (-jnp.arange(0, axis_dim, 2, dtype=jnp.float32) / axis_dim)
    ang_r = row_pos[:, None] * inv_freq[None]
    ang_c = col_pos[:, None] * inv_freq[None]
    ang = jnp.concatenate([ang_r, ang_r, ang_c, ang_c], axis=-1)
    return jnp.cos(ang), jnp.sin(ang)


def apply_axial_rope(x, cos, sin):
    xf = x.astype(jnp.float32)
    xs = xf.reshape(*x.shape[:-1], 2, 2, HEAD_DIM // 4)
    rot = jnp.stack([-xs[..., 1, :], xs[..., 0, :]], axis=-2).reshape(x.shape)
    return (xf * cos[None, :, None, :] + rot * sin[None, :, None, :]).astype(x.dtype)


def gqa_mixer(q, k, v, q_gain, k_gain):
    B, S, _ = q.shape
    G = B_HEADS // B_KV_HEADS
    q = rms_norm(q.reshape(B, S, B_HEADS, HEAD_DIM), q_gain)
    k = rms_norm(k.reshape(B, S, B_KV_HEADS, HEAD_DIM), k_gain)
    v = v.reshape(B, S, B_KV_HEADS, HEAD_DIM)
    cos, sin = axial_rope_tables(S)
    q = apply_axial_rope(q, cos, sin)
    k = apply_axial_rope(k, cos, sin)
    scale = 1.0 / math.sqrt(HEAD_DIM)
    qb = q.reshape(B, S // Q_BLOCK, Q_BLOCK, B_KV_HEADS, G, HEAD_DIM).transpose(1, 0, 2, 3, 4, 5)

    def attend_block(qi):
        s = jnp.einsum('bqkgd,bskd->bkgqs', qi, k).astype(jnp.float32) * scale
        p = jax.nn.softmax(s, axis=-1).astype(v.dtype)
        return jnp.einsum('bkgqs,bskd->bqkgd', p, v)

    o = lax.map(attend_block, qb)
    return o.transpose(1, 0, 2, 3, 4, 5).reshape(B, S, B_WIDTH)


def encoder_layer(x, w_in, b_in, q_gain, k_gain, w_proj_a, w_proj_b, w_out, ln_g, ln_b, slopes):
    h = jnp.einsum('bsd,de->bse', x, w_in) + b_in
    splits = [int(i) for i in np.cumsum(IN_WIDTHS)[:-1]]
    aq, ak, av, ag, bq, bk, bv, bg, gate_a, gate_b = jnp.split(h, splits, axis=-1)
    ya = dilated_mixer(aq, ak, av, slopes) * jax.nn.silu(ag)
    yb = gqa_mixer(bq, bk, bv, q_gain, k_gain) * jax.nn.silu(bg)
    merged = (jax.nn.sigmoid(gate_a) * jnp.einsum('bse,ed->bsd', ya, w_proj_a)
              + jax.nn.sigmoid(gate_b) * jnp.einsum('bse,ed->bsd', yb, w_proj_b))
    sub = jnp.einsum('bsd,de->bse', merged, w_out)
    return layer_norm(ALPHA * x + sub, ln_g, ln_b)


def run_trunk(x, w_in, b_in, q_gain, k_gain, w_proj_a, w_proj_b, w_out, ln_g, ln_b):
    slopes = alibi_slopes()
    for l in range(DEPTH):
        x = encoder_layer(x, w_in[l], b_in[l], q_gain[l], k_gain[l], w_proj_a[l], w_proj_b[l],
                          w_out[l], ln_g[l], ln_b[l], slopes)
    return x


def setup_inputs(seed: int = 0) -> dict:
    key = jax.random.key(seed)
    ks = jax.random.split(key, 12)
    f32 = jnp.float32
    x_prompt = jax.random.normal(ks[0], (BATCH, SEQ, D_MODEL), f32)
    x_sample = jax.random.normal(ks[1], (DEC_BATCH, DEC_SEQ, D_MODEL), f32)
    w_in = jax.random.normal(ks[2], (DEPTH, D_MODEL, IN_WIDTH), f32) * D_MODEL ** -0.5
    b_in = jax.random.normal(ks[3], (DEPTH, IN_WIDTH), f32) * 0.02
    q_gain = 1.0 + 0.05 * jax.random.normal(ks[4], (DEPTH, HEAD_DIM), f32)
    k_gain = 1.0 + 0.05 * jax.random.normal(ks[5], (DEPTH, HEAD_DIM), f32)
    w_proj_a = jax.random.normal(ks[6], (DEPTH, A_WIDTH, D_MODEL), f32) * (A_WIDTH ** -0.5 * BETA)
    w_proj_b = jax.random.normal(ks[7], (DEPTH, B_WIDTH, D_MODEL), f32) * (B_WIDTH ** -0.5 * BETA)
    w_out = jax.random.normal(ks[8], (DEPTH, D_MODEL, D_MODEL), f32) * (D_MODEL ** -0.5 * BETA)
    ln_g = 1.0 + 0.05 * jax.random.normal(ks[9], (DEPTH, D_MODEL), f32)
    ln_b = 0.02 * jax.random.normal(ks[10], (DEPTH, D_MODEL), f32)
    return {"x_prompt": x_prompt, "x_sample": x_sample, "w_in": w_in, "b_in": b_in,
            "q_gain": q_gain, "k_gain": k_gain, "w_proj_a": w_proj_a, "w_proj_b": w_proj_b,
            "w_out": w_out, "ln_g": ln_g, "ln_b": ln_b}


def reference(x_prompt, x_sample, w_in, b_in, q_gain, k_gain, w_proj_a, w_proj_b, w_out, ln_g, ln_b):
    y_prompt = run_trunk(x_prompt, w_in, b_in, q_gain, k_gain, w_proj_a, w_proj_b, w_out, ln_g, ln_b)
    y_sample = run_trunk(x_sample, w_in, b_in, q_gain, k_gain, w_proj_a, w_proj_b, w_out, ln_g, ln_b)
    return (y_prompt, y_sample)
```

```python
import functools
import math

import jax
import jax.numpy as jnp
import numpy as np
from jax import lax
from jax.experimental import pallas as pl
from jax.experimental.pallas import tpu as pltpu

F32 = jnp.float32
BF16 = jnp.bfloat16

D_MODEL = 1024
DEPTH = 2
HEAD_DIM = 128
GRID_W = 64
A_GROUPS = ((128, 1), (512, 4), (2048, 16))
A_HEADS_PER_GROUP = 4
A_HEADS = 12
A_WIDTH = 512
ALIBI_MAX_BIAS = 8.0
B_HEADS = 8
B_KV_HEADS = 2
B_GROUP = B_HEADS // B_KV_HEADS
ROPE_THETA = 10000.0
IN_WIDTH = 9728
RMS_EPS = 1e-6
LN_EPS = 1e-5
MASK_VALUE = -1e30
ALPHA = (2 * DEPTH) ** 0.25
QK_SCALE = 1.0 / math.sqrt(HEAD_DIM)

COL_TILE = 512
N_COL_TILES = IN_WIDTH // COL_TILE
T_AQ, T_AK, T_AV, T_AG = 0, 3, 6, 9
T_BQ, T_BKV, T_BG = 10, 12, 13
T_GATE_A, T_GATE_B = 15, 17
BK_COL128 = 6144 // HEAD_DIM
BV_COL128 = 6400 // HEAD_DIM

N_SIDE = 64

VMEM_LIMIT_BYTES = 56 * 1024 * 1024

IN_PROJ_ROWS = 2048
FLASH_Q_ROWS = 256
FLASH_K_ROWS = 512
DIL_ROWS = 256
DIL_SUB = 128
POST_ROWS = 512


def _sigmoid(x):
    return 1.0 / (1.0 + jnp.exp(-x))


def _in_proj_kernel(x_ref, w_ref, b_ref, qg_ref, kg_ref, cos_ref, sina_ref, sinb_ref, o_ref, xb_ref):
    j = pl.program_id(1)

    @pl.when(j == 0)
    def _():
        xb_ref[...] = x_ref[...].astype(BF16)

    h = jnp.dot(xb_ref[...], w_ref[...], preferred_element_type=F32) + b_ref[...]

    def rms_rope(hh, gain):
        ms = jnp.mean(hh * hh, axis=-1, keepdims=True)
        y = hh * lax.rsqrt(ms + RMS_EPS) * gain
        return (y * cos_ref[...] + pltpu.roll(y, 96, 1) * sina_ref[...]
                + pltpu.roll(y, 32, 1) * sinb_ref[...])

    @pl.when(j < T_AK)
    def _():
        o_ref[...] = (h * QK_SCALE).astype(BF16)

    @pl.when((j >= T_AK) & (j < T_AG))
    def _():
        o_ref[...] = h.astype(BF16)

    @pl.when((j == T_AG) | (j == T_BG) | (j == T_BG + 1))
    def _():
        o_ref[...] = (h * _sigmoid(h)).astype(BF16)

    @pl.when((j == T_BQ) | (j == T_BQ + 1))
    def _():
        for hd in range(COL_TILE // HEAD_DIM):
            sl = slice(hd * HEAD_DIM, (hd + 1) * HEAD_DIM)
            o_ref[:, sl] = (rms_rope(h[:, sl], qg_ref[...]) * QK_SCALE).astype(BF16)

    @pl.when(j == T_BKV)
    def _():
        for hd in range(B_KV_HEADS):
            sl = slice(hd * HEAD_DIM, (hd + 1) * HEAD_DIM)
            o_ref[:, sl] = rms_rope(h[:, sl], kg_ref[...]).astype(BF16)
        o_ref[:, B_KV_HEADS * HEAD_DIM:] = h[:, B_KV_HEADS * HEAD_DIM:].astype(BF16)

    @pl.when(j >= T_GATE_A)
    def _():
        o_ref[...] = _sigmoid(h).astype(BF16)


def _in_proj(x, w, b, q_gain, k_gain, cos, sina, sinb, seq):
    tokens = x.shape[0]
    tm = min(IN_PROJ_ROWS, seq)
    pos_blocks = seq // tm
    row_spec = lambda width: pl.BlockSpec((tm, width), lambda i, j: (i, 0))
    pos_spec = pl.BlockSpec((tm, HEAD_DIM), lambda i, j: (i % pos_blocks, 0))
    vec_spec = pl.BlockSpec((1, HEAD_DIM), lambda i, j: (0, 0))
    return pl.pallas_call(
        _in_proj_kernel,
        grid=(tokens // tm, N_COL_TILES),
        in_specs=[
            row_spec(D_MODEL),
            pl.BlockSpec((D_MODEL, COL_TILE), lambda i, j: (0, j)),
            pl.BlockSpec((1, COL_TILE), lambda i, j: (0, j)),
            vec_spec, vec_spec, pos_spec, pos_spec, pos_spec,
        ],
        out_specs=pl.BlockSpec((tm, COL_TILE), lambda i, j: (i, j)),
        out_shape=jax.ShapeDtypeStruct((tokens, IN_WIDTH), BF16),
        scratch_shapes=[pltpu.VMEM((tm, D_MODEL), BF16)],
        compiler_params=pltpu.CompilerParams(
            dimension_semantics=("arbitrary", "arbitrary"), vmem_limit_bytes=VMEM_LIMIT_BYTES),
        name="in_proj",
    )(x, w, b, q_gain, k_gain, cos, sina, sinb)


def _flash_kernel(q_ref, k_ref, v_ref, g_ref, o_ref, q4_ref, m_ref, l_ref, acc_ref, *, tq, tk, n_chunks):
    for hd in range(B_GROUP):
        q4_ref[hd * tq:(hd + 1) * tq, :] = q_ref[:, hd * HEAD_DIM:(hd + 1) * HEAD_DIM]
    m_ref[...] = jnp.full(m_ref.shape, -jnp.inf, F32)
    l_ref[...] = jnp.zeros(l_ref.shape, F32)
    acc_ref[...] = jnp.zeros(acc_ref.shape, F32)

    def body(c, carry):
        start = pl.multiple_of(c * tk, tk)
        k = k_ref[pl.ds(start, tk), :]
        v = v_ref[pl.ds(start, tk), :]
        s = lax.dot_general(q4_ref[...], k, (((1,), (1,)), ((), ())), preferred_element_type=F32)
        m_prev = m_ref[...]
        m_new = jnp.maximum(m_prev, jnp.max(s, axis=-1, keepdims=True))
        alpha = jnp.exp(m_prev - m_new)
        p = jnp.exp(s - m_new)
        l_ref[...] = alpha * l_ref[...] + jnp.sum(p, axis=-1, keepdims=True)
        acc_ref[...] = alpha * acc_ref[...] + jnp.dot(p.astype(BF16), v, preferred_element_type=F32)
        m_ref[...] = m_new
        return carry

    lax.fori_loop(0, n_chunks, body, 0)

    o = acc_ref[...] / l_ref[...]
    for hd in range(B_GROUP):
        sl = slice(hd * HEAD_DIM, (hd + 1) * HEAD_DIM)
        o_ref[:, sl] = (o[hd * tq:(hd + 1) * tq, :] * g_ref[:, sl].astype(F32)).astype(BF16)


def _gqa_flash(h3):
    batch, seq, _ = h3.shape
    tq, tk = FLASH_Q_ROWS, FLASH_K_ROWS
    kernel = functools.partial(_flash_kernel, tq=tq, tk=tk, n_chunks=seq // tk)
    return pl.pallas_call(
        kernel,
        grid=(batch, B_KV_HEADS, seq // tq),
        in_specs=[
            pl.BlockSpec((None, tq, COL_TILE), lambda b, g, i: (b, i, T_BQ + g)),
            pl.BlockSpec((None, seq, HEAD_DIM), lambda b, g, i: (b, 0, BK_COL128 + g)),
            pl.BlockSpec((None, seq, HEAD_DIM), lambda b, g, i: (b, 0, BV_COL128 + g)),
            pl.BlockSpec((None, tq, COL_TILE), lambda b, g, i: (b, i, T_BG + g)),
        ],
        out_specs=pl.BlockSpec((None, tq, COL_TILE), lambda b, g, i: (b, i, g)),
        out_shape=jax.ShapeDtypeStruct((batch, seq, B_HEADS * HEAD_DIM), BF16),
        scratch_shapes=[
            pltpu.VMEM((B_GROUP * tq, HEAD_DIM), BF16),
            pltpu.VMEM((B_GROUP * tq, 1), F32),
            pltpu.VMEM((B_GROUP * tq, 1), F32),
            pltpu.VMEM((B_GROUP * tq, HEAD_DIM), F32),
        ],
        compiler_params=pltpu.CompilerParams(
            dimension_semantics=("arbitrary", "arbitrary", "arbitrary"), vmem_limit_bytes=VMEM_LIMIT_BYTES),
        name="gqa_flash",
    )(h3, h3, h3, h3)


def _dilated_kernel(q_ref, kp_ref, kc_ref, kn_ref, vp_ref, vc_ref, vn_ref, bias_ref, o_ref, lse_ref,
                    kw_ref, vw_ref, *, tl, length):
    i = pl.program_id(2)
    kw_ref[0:N_SIDE, :] = kp_ref[...]
    kw_ref[N_SIDE:N_SIDE + tl, :] = kc_ref[...]
    kw_ref[N_SIDE + tl:, :] = kn_ref[...]
    vw_ref[0:N_SIDE, :] = vp_ref[...]
    vw_ref[N_SIDE:N_SIDE + tl, :] = vc_ref[...]
    vw_ref[N_SIDE + tl:, :] = vn_ref[...]

    win = DIL_SUB + 2 * N_SIDE
    for a in range(tl // DIL_SUB):
        r0 = a * DIL_SUB
        key_pos = i * tl + (r0 - N_SIDE) + lax.broadcasted_iota(jnp.int32, (1, win), 1)
        key_ok = (key_pos >= 0) & (key_pos < length)
        lse_cols = []
        for hd in range(A_HEADS_PER_GROUP):
            sl = slice(hd * HEAD_DIM, (hd + 1) * HEAD_DIM)
            q = q_ref[r0:r0 + DIL_SUB, sl]
            k = kw_ref[r0:r0 + win, sl]
            v = vw_ref[r0:r0 + win, sl]
            s = lax.dot_general(q, k, (((1,), (1,)), ((), ())), preferred_element_type=F32) + bias_ref[hd]
            s = jnp.where(key_ok, s, MASK_VALUE)
            m = jnp.max(s, axis=-1, keepdims=True)
            p = jnp.exp(s - m)
            l = jnp.sum(p, axis=-1, keepdims=True)
            o = jnp.dot(p.astype(BF16), v, preferred_element_type=F32) / l
            o_ref[r0:r0 + DIL_SUB, sl] = o.astype(BF16)
            lse_cols.append(jnp.broadcast_to(m + jnp.log(l), (DIL_SUB, HEAD_DIM // A_HEADS_PER_GROUP)))
        lse_ref[r0:r0 + DIL_SUB, :] = jnp.concatenate(lse_cols, axis=-1)


def _dilated_group(h3, bias, group, dil):
    batch, seq, _ = h3.shape
    length = seq // dil
    tl = min(DIL_ROWS, length)
    halo_per_block = tl // N_SIDE
    n_halo_blocks = length // N_SIDE
    hv = h3.reshape(batch, length, dil * IN_WIDTH)
    col = lambda r, tile: r * N_COL_TILES + tile + group

    def cur(tile):
        return pl.BlockSpec((None, tl, COL_TILE), lambda b, r, i: (b, i, col(r, tile)))

    def prev(tile):
        return pl.BlockSpec((None, N_SIDE, COL_TILE),
                            lambda b, r, i: (b, jnp.maximum(i * halo_per_block - 1, 0), col(r, tile)))

    def nxt(tile):
        return pl.BlockSpec((None, N_SIDE, COL_TILE),
                            lambda b, r, i: (b, jnp.minimum((i + 1) * halo_per_block, n_halo_blocks - 1),
                                             col(r, tile)))

    kernel = functools.partial(_dilated_kernel, tl=tl, length=length)
    o, lse = pl.pallas_call(
        kernel,
        grid=(batch, dil, length // tl),
        in_specs=[
            cur(T_AQ), prev(T_AK), cur(T_AK), nxt(T_AK), prev(T_AV), cur(T_AV), nxt(T_AV),
            pl.BlockSpec((A_HEADS_PER_GROUP, DIL_SUB, DIL_SUB + 2 * N_SIDE), lambda b, r, i: (0, 0, 0)),
        ],
        out_specs=[
            pl.BlockSpec((None, tl, A_WIDTH), lambda b, r, i: (b, i, r)),
            pl.BlockSpec((None, tl, HEAD_DIM), lambda b, r, i: (b, i, r)),
        ],
        out_shape=[
            jax.ShapeDtypeStruct((batch, length, dil * A_WIDTH), BF16),
            jax.ShapeDtypeStruct((batch, length, dil * HEAD_DIM), F32),
        ],
        scratch_shapes=[
            pltpu.VMEM((tl + 2 * N_SIDE, COL_TILE), BF16),
            pltpu.VMEM((tl + 2 * N_SIDE, COL_TILE), BF16),
        ],
        compiler_params=pltpu.CompilerParams(
            dimension_semantics=("arbitrary", "arbitrary", "arbitrary"), vmem_limit_bytes=VMEM_LIMIT_BYTES),
        name=f"dilated_g{group}",
    )(hv, hv, hv, hv, hv, hv, hv, bias)
    return o.reshape(batch * seq, A_WIDTH), lse.reshape(batch * seq, HEAD_DIM)


def _post_kernel(x_ref, o0_ref, o1_ref, o2_ref, l0_ref, l1_ref, l2_ref, ag_ref, yb_ref,
                 ga0_ref, ga1_ref, gb0_ref, gb1_ref, wpa_ref, wpb_ref, wo_ref, lng_ref, lnb_ref, y_ref):
    l0, l1, l2 = l0_ref[...], l1_ref[...], l2_ref[...]
    top = jnp.maximum(jnp.maximum(l0, l1), l2)
    e0, e1, e2 = jnp.exp(l0 - top), jnp.exp(l1 - top), jnp.exp(l2 - top)
    inv = 1.0 / (e0 + e1 + e2)
    lanes_per_head = HEAD_DIM // A_HEADS_PER_GROUP
    rows = x_ref.shape[0]
    ya_heads = []
    for hd in range(A_HEADS_PER_GROUP):
        sl = slice(hd * HEAD_DIM, (hd + 1) * HEAD_DIM)
        pick = slice(hd * lanes_per_head, hd * lanes_per_head + 1)
        w0 = jnp.broadcast_to((e0 * inv)[:, pick], (rows, HEAD_DIM))
        w1 = jnp.broadcast_to((e1 * inv)[:, pick], (rows, HEAD_DIM))
        w2 = jnp.broadcast_to((e2 * inv)[:, pick], (rows, HEAD_DIM))
        mix = (w0 * o0_ref[:, sl].astype(F32) + w1 * o1_ref[:, sl].astype(F32)
               + w2 * o2_ref[:, sl].astype(F32))
        ya_heads.append((mix * ag_ref[:, sl].astype(F32)).astype(BF16))
    ya = jnp.concatenate(ya_heads, axis=-1)
    pa = jnp.dot(ya, wpa_ref[...], preferred_element_type=F32)
    pb = jnp.dot(yb_ref[...], wpb_ref[...], preferred_element_type=F32)
    gate_a = jnp.concatenate([ga0_ref[...], ga1_ref[...]], axis=-1).astype(F32)
    gate_b = jnp.concatenate([gb0_ref[...], gb1_ref[...]], axis=-1).astype(F32)
    merged = (gate_a * pa + gate_b * pb).astype(BF16)
    z = ALPHA * x_ref[...] + jnp.dot(merged, wo_ref[...], preferred_element_type=F32)
    mu = jnp.mean(z, axis=-1, keepdims=True)
    zc = z - mu
    var = jnp.mean(zc * zc, axis=-1, keepdims=True)
    y_ref[...] = zc * lax.rsqrt(var + LN_EPS) * lng_ref[...] + lnb_ref[...]


def _post(x, outs, lses, h, yb, wpa, wpb, wo, ln_g, ln_b):
    tokens = x.shape[0]
    tm = POST_ROWS
    row = lambda width, col=0: pl.BlockSpec((tm, width), lambda i, col=col: (i, col))
    full = lambda a: pl.BlockSpec(a.shape, lambda i: (0, 0))
    return pl.pallas_call(
        _post_kernel,
        grid=(tokens // tm,),
        in_specs=[
            row(D_MODEL),
            row(A_WIDTH), row(A_WIDTH), row(A_WIDTH),
            row(HEAD_DIM), row(HEAD_DIM), row(HEAD_DIM),
            row(COL_TILE, T_AG),
            row(B_HEADS * HEAD_DIM),
            row(COL_TILE, T_GATE_A), row(COL_TILE, T_GATE_A + 1),
            row(COL_TILE, T_GATE_B), row(COL_TILE, T_GATE_B + 1),
            full(wpa), full(wpb), full(wo), full(ln_g), full(ln_b),
        ],
        out_specs=row(D_MODEL),
        out_shape=jax.ShapeDtypeStruct((tokens, D_MODEL), F32),
        compiler_params=pltpu.CompilerParams(
            dimension_semantics=("arbitrary",), vmem_limit_bytes=VMEM_LIMIT_BYTES),
        name="post",
    )(x, *outs, *lses, h, yb, h, h, h, h, wpa, wpb, wo, ln_g, ln_b)


def _rope_tables(seq):
    rows = seq // GRID_W
    row_pos = jnp.repeat(jnp.arange(rows), GRID_W).astype(F32)
    col_pos = jnp.tile(jnp.arange(GRID_W), rows).astype(F32)
    axis_dim = HEAD_DIM // 2
    inv_freq = ROPE_THETA ** (-jnp.arange(0, axis_dim, 2, dtype=F32) / axis_dim)
    ang_r = row_pos[:, None] * inv_freq[None]
    ang_c = col_pos[:, None] * inv_freq[None]
    ang = jnp.concatenate([ang_r, ang_r, ang_c, ang_c], axis=-1)
    cos, sin = jnp.cos(ang), jnp.sin(ang)
    first_half = (jnp.arange(HEAD_DIM) % (HEAD_DIM // 2)) < (HEAD_DIM // 4)
    sina = jnp.where(first_half[None], -sin, 0.0)
    sinb = jnp.where(first_half[None], 0.0, sin)
    return cos, sina, sinb


def _alibi_bias(group, dil):
    slopes = 2.0 ** (-ALIBI_MAX_BIAS * np.arange(1, A_HEADS + 1) / A_HEADS)
    slopes = slopes[group * A_HEADS_PER_GROUP:(group + 1) * A_HEADS_PER_GROUP]
    rel = np.arange(DIL_SUB + 2 * N_SIDE)[None, :] - N_SIDE - np.arange(DIL_SUB)[:, None]
    inside = np.abs(rel) <= N_SIDE
    bias = -slopes[:, None, None] * (dil * np.abs(rel)).astype(np.float32)[None]
    return jnp.asarray(np.where(inside[None], bias, MASK_VALUE), dtype=F32)


def _trunk(x, params, biases):
    batch, seq, _ = x.shape
    cos, sina, sinb = _rope_tables(seq)
    xf = x.reshape(batch * seq, D_MODEL)
    for layer in range(DEPTH):
        w_in, b_in, q_gain, k_gain, wpa, wpb, wo, ln_g, ln_b = (p[layer] for p in params)
        h = _in_proj(xf, w_in, b_in, q_gain, k_gain, cos, sina, sinb, seq)
        h3 = h.reshape(batch, seq, IN_WIDTH)
        yb = _gqa_flash(h3).reshape(batch * seq, B_HEADS * HEAD_DIM)
        outs, lses = [], []
        for group, (_, dil) in enumerate(A_GROUPS):
            o, lse = _dilated_group(h3, biases[group], group, dil)
            outs.append(o)
            lses.append(lse)
        xf = _post(xf, outs, lses, h, yb, wpa, wpb, wo, ln_g, ln_b)
    return xf.reshape(batch, seq, D_MODEL)


def kernel(x_prompt, x_sample, w_in, b_in, q_gain, k_gain, w_proj_a, w_proj_b, w_out, ln_g, ln_b):
    params = (
        w_in.astype(BF16),
        b_in.reshape(DEPTH, 1, IN_WIDTH),
        q_gain.reshape(DEPTH, 1, HEAD_DIM),
        k_gain.reshape(DEPTH, 1, HEAD_DIM),
        w_proj_a.astype(BF16),
        w_proj_b.astype(BF16),
        w_out.astype(BF16),
        ln_g.reshape(DEPTH, 1, D_MODEL),
        ln_b.reshape(DEPTH, 1, D_MODEL),
    )
    biases = [_alibi_bias(group, dil) for group, (_, dil) in enumerate(A_GROUPS)]
    return (_trunk(x_prompt, params, biases), _trunk(x_sample, params, biases))
```

```python
import functools
import math

import jax
import jax.numpy as jnp
import numpy as np
from jax import lax
from jax.experimental import pallas as pl
from jax.experimental.pallas import tpu as pltpu

F32 = jnp.float32
BF16 = jnp.bfloat16

D_MODEL = 1024
DEPTH = 2
HEAD_DIM = 128
GRID_W = 64
A_GROUPS = ((128, 1), (512, 4), (2048, 16))
A_HEADS_PER_GROUP = 4
A_HEADS = 12
A_WIDTH = 512
ALIBI_MAX_BIAS = 8.0
B_HEADS = 8
B_KV_HEADS = 2
B_GROUP = B_HEADS // B_KV_HEADS
ROPE_THETA = 10000.0
IN_WIDTH = 9728
RMS_EPS = 1e-6
LN_EPS = 1e-5
MASK_VALUE = -1e30
ALPHA = (2 * DEPTH) ** 0.25
QK_SCALE = 1.0 / math.sqrt(HEAD_DIM)
BQ_SCALE = QK_SCALE * math.log2(math.e)
ONES_ROWS = 16
LOGIT_BOUND = 60.0

COL_TILE = 512
N_COL_TILES = IN_WIDTH // COL_TILE
T_AQ, T_AK, T_AV, T_AG = 0, 3, 6, 9
T_BQ, T_BKV, T_BG = 10, 12, 13
T_GATE_A, T_GATE_B = 15, 17
BK_COL128 = 6144 // HEAD_DIM
BV_COL128 = 6400 // HEAD_DIM

N_SIDE = 64

VMEM_LIMIT_BYTES = 56 * 1024 * 1024

SUPER_TILE = 2048
FLASH_Q_ROWS = 256
FLASH_K_ROWS = 512
DIL_SUB = 128
POST_ROWS = 512


def _sigmoid(x):
    return 1.0 / (1.0 + jnp.exp(-x))


def _in_proj_kernel(x_ref, w_ref, b_ref, qg_ref, kg_ref, cos_ref, sina_ref, sinb_ref, o_ref, xb_ref, hs_ref):
    j = pl.program_id(1)
    tm = x_ref.shape[0]

    @pl.when(j == 0)
    def _():
        xb_ref[...] = x_ref[...].astype(BF16)

    h = jnp.dot(xb_ref[...], w_ref[...], preferred_element_type=F32) + b_ref[...]

    def rms_rope(hh, gain):
        ms = jnp.mean(hh * hh, axis=-1, keepdims=True)
        y = hh * lax.rsqrt(ms + RMS_EPS) * gain
        return (y * cos_ref[...] + pltpu.roll(y, 96, 1) * sina_ref[...]
                + pltpu.roll(y, 32, 1) * sinb_ref[...])

    a_scale = jnp.where(j < T_AK, QK_SCALE, 1.0)
    a_group = j % len(A_GROUPS)

    @pl.when((j < T_AG) & (a_group == 0))
    def _():
        o_ref[...] = (h * a_scale).astype(BF16)

    @pl.when((j < T_AG) & (a_group > 0))
    def _():
        for ln in range(COL_TILE // HEAD_DIM):
            hs_ref[ln] = h[:, ln * HEAD_DIM:(ln + 1) * HEAD_DIM] * a_scale

        def store_class_major(dil):
            n = tm // dil
            for c in range(dil):
                for ln in range(COL_TILE // HEAD_DIM):
                    o_ref[c * n:(c + 1) * n, ln * HEAD_DIM:(ln + 1) * HEAD_DIM] = (
                        hs_ref[ln, pl.ds(c, n, stride=dil), :].astype(BF16))

        for group, (_, dil) in enumerate(A_GROUPS):
            if dil > 1:
                pl.when(a_group == group)(functools.partial(store_class_major, dil))

    @pl.when((j == T_AG) | (j == T_BG) | (j == T_BG + 1))
    def _():
        o_ref[...] = (h * _sigmoid(h)).astype(BF16)

    @pl.when((j == T_BQ) | (j == T_BQ + 1))
    def _():
        for hd in range(COL_TILE // HEAD_DIM):
            sl = slice(hd * HEAD_DIM, (hd + 1) * HEAD_DIM)
            o_ref[:, sl] = (rms_rope(h[:, sl], qg_ref[...]) * BQ_SCALE).astype(BF16)

    @pl.when(j == T_BKV)
    def _():
        for hd in range(B_KV_HEADS):
            sl = slice(hd * HEAD_DIM, (hd + 1) * HEAD_DIM)
            o_ref[:, sl] = rms_rope(h[:, sl], kg_ref[...]).astype(BF16)
        o_ref[:, B_KV_HEADS * HEAD_DIM:] = h[:, B_KV_HEADS * HEAD_DIM:].astype(BF16)

    @pl.when(j >= T_GATE_A)
    def _():
        o_ref[...] = _sigmoid(h).astype(BF16)


def _in_proj(x, w, b, q_gain, k_gain, cos, sina, sinb, seq):
    tokens = x.shape[0]
    tm = SUPER_TILE
    assert seq % tm == 0
    pos_blocks = seq // tm
    row_spec = lambda width: pl.BlockSpec((tm, width), lambda i, j: (i, 0))
    pos_spec = pl.BlockSpec((tm, HEAD_DIM), lambda i, j: (i % pos_blocks, 0))
    vec_spec = pl.BlockSpec((1, HEAD_DIM), lambda i, j: (0, 0))
    return pl.pallas_call(
        _in_proj_kernel,
        grid=(tokens // tm, N_COL_TILES),
        in_specs=[
            row_spec(D_MODEL),
            pl.BlockSpec((D_MODEL, COL_TILE), lambda i, j: (0, j)),
            pl.BlockSpec((1, COL_TILE), lambda i, j: (0, j)),
            vec_spec, vec_spec, pos_spec, pos_spec, pos_spec,
        ],
        out_specs=pl.BlockSpec((tm, COL_TILE), lambda i, j: (i, j)),
        out_shape=jax.ShapeDtypeStruct((tokens, IN_WIDTH), BF16),
        scratch_shapes=[
            pltpu.VMEM((tm, D_MODEL), BF16),
            pltpu.VMEM((COL_TILE // HEAD_DIM, tm, HEAD_DIM), F32),
        ],
        compiler_params=pltpu.CompilerParams(
            dimension_semantics=("arbitrary", "arbitrary"), vmem_limit_bytes=VMEM_LIMIT_BYTES),
        name="in_proj",
    )(x, w, b, q_gain, k_gain, cos, sina, sinb)


def _flash_kernel(q_ref, k_ref, v_ref, g_ref, o_ref, qt_ref, vt_ref, m_ref, acc_ref, k2max_ref,
                  *, tq, tk, n_chunks):
    i = pl.program_id(2)

    @pl.when(i == 0)
    def _():
        def fill(c, k2max):
            start = pl.multiple_of(c * tk, tk)
            vt_ref[c, 0:HEAD_DIM, :] = v_ref[pl.ds(start, tk), :].astype(F32).T.astype(BF16)
            vt_ref[c, HEAD_DIM:, :] = jnp.ones((ONES_ROWS, tk), BF16)
            kf = k_ref[pl.ds(start, tk), :].astype(F32)
            k2 = jnp.max(jnp.sum(kf * kf, axis=-1, keepdims=True), axis=0, keepdims=True)
            return jnp.maximum(k2max, k2)

        k2max_ref[...] = lax.fori_loop(0, n_chunks, fill, jnp.zeros((1, 1), F32))

    q2 = jnp.zeros((1, 1), F32)
    for hd in range(B_GROUP):
        qf = q_ref[:, hd * HEAD_DIM:(hd + 1) * HEAD_DIM].astype(F32).T
        qt_ref[:, hd * tq:(hd + 1) * tq] = qf.astype(BF16)
        q2 = jnp.maximum(q2, jnp.max(jnp.sum(qf * qf, axis=0, keepdims=True), axis=1, keepdims=True))
    acc_ref[...] = jnp.zeros(acc_ref.shape, F32)

    logits_bounded = (q2 * k2max_ref[...])[0, 0] <= LOGIT_BOUND * LOGIT_BOUND

    def scores(c):
        start = pl.multiple_of(c * tk, tk)
        return jnp.dot(k_ref[pl.ds(start, tk), :], qt_ref[...], preferred_element_type=F32)

    def bounded_body(c, carry):
        pt = jnp.exp2(scores(c)).astype(BF16)
        acc_ref[...] += jnp.dot(vt_ref[c], pt, preferred_element_type=F32)
        return carry

    def online_body(c, carry):
        st = scores(c)
        m_prev = m_ref[...]
        m_new = jnp.maximum(m_prev, jnp.max(st, axis=0, keepdims=True))
        alpha = jnp.exp2(m_prev - m_new)
        pt = jnp.exp2(st - m_new).astype(BF16)
        acc_ref[...] = alpha * acc_ref[...] + jnp.dot(vt_ref[c], pt, preferred_element_type=F32)
        m_ref[...] = m_new
        return carry

    @pl.when(logits_bounded)
    def _():
        lax.fori_loop(0, n_chunks, bounded_body, 0, unroll=2)

    @pl.when(jnp.logical_not(logits_bounded))
    def _():
        m_ref[...] = jnp.full(m_ref.shape, MASK_VALUE, F32)
        lax.fori_loop(0, n_chunks, online_body, 0, unroll=2)

    acc = acc_ref[...]
    ot = acc[0:HEAD_DIM, :] / acc[HEAD_DIM:HEAD_DIM + 1, :]
    for hd in range(B_GROUP):
        sl = slice(hd * HEAD_DIM, (hd + 1) * HEAD_DIM)
        o_ref[:, sl] = (ot[:, hd * tq:(hd + 1) * tq].T * g_ref[:, sl].astype(F32)).astype(BF16)


def _gqa_flash(h3):
    batch, seq, _ = h3.shape
    tq, tk = FLASH_Q_ROWS, FLASH_K_ROWS
    n_chunks = seq // tk
    kernel = functools.partial(_flash_kernel, tq=tq, tk=tk, n_chunks=n_chunks)
    return pl.pallas_call(
        kernel,
        grid=(batch, B_KV_HEADS, seq // tq),
        in_specs=[
            pl.BlockSpec((None, tq, COL_TILE), lambda b, g, i: (b, i, T_BQ + g)),
            pl.BlockSpec((None, seq, HEAD_DIM), lambda b, g, i: (b, 0, BK_COL128 + g)),
            pl.BlockSpec((None, seq, HEAD_DIM), lambda b, g, i: (b, 0, BV_COL128 + g)),
            pl.BlockSpec((None, tq, COL_TILE), lambda b, g, i: (b, i, T_BG + g)),
        ],
        out_specs=pl.BlockSpec((None, tq, COL_TILE), lambda b, g, i: (b, i, g)),
        out_shape=jax.ShapeDtypeStruct((batch, seq, B_HEADS * HEAD_DIM), BF16),
        scratch_shapes=[
            pltpu.VMEM((HEAD_DIM, B_GROUP * tq), BF16),
            pltpu.VMEM((n_chunks, HEAD_DIM + ONES_ROWS, tk), BF16),
            pltpu.VMEM((1, B_GROUP * tq), F32),
            pltpu.VMEM((HEAD_DIM + ONES_ROWS, B_GROUP * tq), F32),
            pltpu.VMEM((1, 1), F32),
        ],
        compiler_params=pltpu.CompilerParams(
            dimension_semantics=("arbitrary", "arbitrary", "arbitrary"), vmem_limit_bytes=VMEM_LIMIT_BYTES),
        name="gqa_flash",
    )(h3, h3, h3, h3)


def _strided_rows(start, size, stride):
    return pl.ds(start, size) if stride == 1 else pl.ds(start, size, stride=stride)


def _aligned(index, multiple):
    return index if isinstance(index, int) else pl.multiple_of(index, multiple)


def _dilated_kernel(q_ref, k_ref, v_ref, kp_ref, kn_ref, vp_ref, vn_ref, bias_ref, o_ref, lse_ref,
                    kw_ref, vw_ref, stage_ref, *, dil, length):
    n = SUPER_TILE // dil
    win = DIL_SUB + 2 * N_SIDE
    first_pos = pl.program_id(1) * n
    lanes_per_head = HEAD_DIM // A_HEADS_PER_GROUP

    def sub_block(c, a):
        r0 = _aligned(a * DIL_SUB, DIL_SUB)
        key_pos = first_pos + a * DIL_SUB - N_SIDE + lax.broadcasted_iota(jnp.int32, (1, win), 1)
        key_ok = (key_pos >= 0) & (key_pos < length)
        token_rows = _strided_rows(c + dil * r0, DIL_SUB, dil)
        lse_cols = []
        for hd in range(A_HEADS_PER_GROUP):
            sl = slice(hd * HEAD_DIM, (hd + 1) * HEAD_DIM)
            q = q_ref[pl.ds(_aligned(c * n + r0, DIL_SUB), DIL_SUB), sl]
            k = kw_ref[pl.ds(r0, win), sl]
            v = vw_ref[pl.ds(r0, win), sl]
            s = lax.dot_general(q, k, (((1,), (1,)), ((), ())), preferred_element_type=F32) + bias_ref[hd]
            s = jnp.where(key_ok, s, MASK_VALUE)
            m = jnp.max(s, axis=-1, keepdims=True)
            p = jnp.exp(s - m)
            l = jnp.sum(p, axis=-1, keepdims=True)
            stage_ref[hd, token_rows, :] = jnp.dot(p.astype(BF16), v, preferred_element_type=F32) / l
            lse_cols.append(jnp.broadcast_to(m + jnp.log(l), (DIL_SUB, lanes_per_head)))
        lse_ref[token_rows, :] = jnp.concatenate(lse_cols, axis=-1)

    def one_class(c, carry):
        rows = pl.ds(_aligned(c * n, n), n)
        kw_ref[0:N_SIDE, :] = kp_ref[c]
        kw_ref[N_SIDE:N_SIDE + n, :] = k_ref[rows, :]
        kw_ref[N_SIDE + n:, :] = kn_ref[c]
        vw_ref[0:N_SIDE, :] = vp_ref[c]
        vw_ref[N_SIDE:N_SIDE + n, :] = v_ref[rows, :]
        vw_ref[N_SIDE + n:, :] = vn_ref[c]
        if n == DIL_SUB:
            sub_block(c, 0)
        else:
            lax.fori_loop(0, n // DIL_SUB, lambda a, inner: (sub_block(c, a), inner)[1], 0)
        return carry

    if dil == 1:
        one_class(0, 0)
    else:
        lax.fori_loop(0, dil, one_class, 0)

    for hd in range(A_HEADS_PER_GROUP):
        o_ref[:, hd * HEAD_DIM:(hd + 1) * HEAD_DIM] = stage_ref[hd].astype(BF16)


def _dilated_group(h3, bias, group, dil):
    batch, seq, _ = h3.shape
    tiles = seq // SUPER_TILE
    n = SUPER_TILE // dil
    halos = n // N_SIDE
    h6 = h3.reshape(batch, tiles, dil, halos, N_SIDE, IN_WIDTH)

    def cur(tile):
        return pl.BlockSpec((None, SUPER_TILE, COL_TILE), lambda b, t: (b, t, tile + group))

    def prev(tile):
        return pl.BlockSpec((None, None, dil, None, N_SIDE, COL_TILE),
                            lambda b, t: (b, jnp.maximum(t - 1, 0), 0, halos - 1, 0, tile + group))

    def nxt(tile):
        return pl.BlockSpec((None, None, dil, None, N_SIDE, COL_TILE),
                            lambda b, t: (b, jnp.minimum(t + 1, tiles - 1), 0, 0, 0, tile + group))

    kernel = functools.partial(_dilated_kernel, dil=dil, length=seq // dil)
    o, lse = pl.pallas_call(
        kernel,
        grid=(batch, tiles),
        in_specs=[
            cur(T_AQ), cur(T_AK), cur(T_AV), prev(T_AK), nxt(T_AK), prev(T_AV), nxt(T_AV),
            pl.BlockSpec((A_HEADS_PER_GROUP, DIL_SUB, DIL_SUB + 2 * N_SIDE), lambda b, t: (0, 0, 0)),
        ],
        out_specs=[
            pl.BlockSpec((None, SUPER_TILE, A_WIDTH), lambda b, t: (b, t, 0)),
            pl.BlockSpec((None, SUPER_TILE, HEAD_DIM), lambda b, t: (b, t, 0)),
        ],
        out_shape=[
            jax.ShapeDtypeStruct((batch, seq, A_WIDTH), BF16),
            jax.ShapeDtypeStruct((batch, seq, HEAD_DIM), F32),
        ],
        scratch_shapes=[
            pltpu.VMEM((n + 2 * N_SIDE, COL_TILE), BF16),
            pltpu.VMEM((n + 2 * N_SIDE, COL_TILE), BF16),
            pltpu.VMEM((A_HEADS_PER_GROUP, SUPER_TILE, HEAD_DIM), F32),
        ],
        compiler_params=pltpu.CompilerParams(
            dimension_semantics=("arbitrary", "arbitrary"), vmem_limit_bytes=VMEM_LIMIT_BYTES),
        name=f"dilated_g{group}",
    )(h3, h3, h3, h6, h6, h6, h6, bias)
    return o.reshape(batch * seq, A_WIDTH), lse.reshape(batch * seq, HEAD_DIM)


def _post_kernel(x_ref, o0_ref, o1_ref, o2_ref, l0_ref, l1_ref, l2_ref, ag_ref, yb_ref,
                 ga0_ref, ga1_ref, gb0_ref, gb1_ref, wpa_ref, wpb_ref, wo_ref, lng_ref, lnb_ref, y_ref):
    l0, l1, l2 = l0_ref[...], l1_ref[...], l2_ref[...]
    top = jnp.maximum(jnp.maximum(l0, l1), l2)
    e0, e1, e2 = jnp.exp(l0 - top), jnp.exp(l1 - top), jnp.exp(l2 - top)
    inv = 1.0 / (e0 + e1 + e2)
    lanes_per_head = HEAD_DIM // A_HEADS_PER_GROUP
    rows = x_ref.shape[0]
    ya_heads = []
    for hd in range(A_HEADS_PER_GROUP):
        sl = slice(hd * HEAD_DIM, (hd + 1) * HEAD_DIM)
        pick = slice(hd * lanes_per_head, hd * lanes_per_head + 1)
        w0 = jnp.broadcast_to((e0 * inv)[:, pick], (rows, HEAD_DIM))
        w1 = jnp.broadcast_to((e1 * inv)[:, pick], (rows, HEAD_DIM))
        w2 = jnp.broadcast_to((e2 * inv)[:, pick], (rows, HEAD_DIM))
        mix = (w0 * o0_ref[:, sl].astype(F32) + w1 * o1_ref[:, sl].astype(F32)
               + w2 * o2_ref[:, sl].astype(F32))
        ya_heads.append((mix * ag_ref[:, sl].astype(F32)).astype(BF16))
    ya = jnp.concatenate(ya_heads, axis=-1)
    pa = jnp.dot(ya, wpa_ref[...], preferred_element_type=F32)
    pb = jnp.dot(yb_ref[...], wpb_ref[...], preferred_element_type=F32)
    gate_a = jnp.concatenate([ga0_ref[...], ga1_ref[...]], axis=-1).astype(F32)
    gate_b = jnp.concatenate([gb0_ref[...], gb1_ref[...]], axis=-1).astype(F32)
    merged = (gate_a * pa + gate_b * pb).astype(BF16)
    z = ALPHA * x_ref[...] + jnp.dot(merged, wo_ref[...], preferred_element_type=F32)
    mu = jnp.mean(z, axis=-1, keepdims=True)
    zc = z - mu
    var = jnp.mean(zc * zc, axis=-1, keepdims=True)
    y_ref[...] = zc * lax.rsqrt(var + LN_EPS) * lng_ref[...] + lnb_ref[...]


def _post(x, outs, lses, h, yb, wpa, wpb, wo, ln_g, ln_b):
    tokens = x.shape[0]
    tm = POST_ROWS
    row = lambda width, col=0: pl.BlockSpec((tm, width), lambda i, col=col: (i, col))
    full = lambda a: pl.BlockSpec(a.shape, lambda i: (0, 0))
    return pl.pallas_call(
        _post_kernel,
        grid=(tokens // tm,),
        in_specs=[
            row(D_MODEL),
            row(A_WIDTH), row(A_WIDTH), row(A_WIDTH),
            row(HEAD_DIM), row(HEAD_DIM), row(HEAD_DIM),
            row(COL_TILE, T_AG),
            row(B_HEADS * HEAD_DIM),
            row(COL_TILE, T_GATE_A), row(COL_TILE, T_GATE_A + 1),
            row(COL_TILE, T_GATE_B), row(COL_TILE, T_GATE_B + 1),
            full(wpa), full(wpb), full(wo), full(ln_g), full(ln_b),
        ],
        out_specs=row(D_MODEL),
        out_shape=jax.ShapeDtypeStruct((tokens, D_MODEL), F32),
        compiler_params=pltpu.CompilerParams(
            dimension_semantics=("arbitrary",), vmem_limit_bytes=VMEM_LIMIT_BYTES),
        name="post",
    )(x, *outs, *lses, h, yb, h, h, h, h, wpa, wpb, wo, ln_g, ln_b)


def _rope_tables(seq):
    rows = seq // GRID_W
    row_pos = jnp.repeat(jnp.arange(rows), GRID_W).astype(F32)
    col_pos = jnp.tile(jnp.arange(GRID_W), rows).astype(F32)
    axis_dim = HEAD_DIM // 2
    inv_freq = ROPE_THETA ** (-jnp.arange(0, axis_dim, 2, dtype=F32) / axis_dim)
    ang_r = row_pos[:, None] * inv_freq[None]
    ang_c = col_pos[:, None] * inv_freq[None]
    ang = jnp.concatenate([ang_r, ang_r, ang_c, ang_c], axis=-1)
    cos, sin = jnp.cos(ang), jnp.sin(ang)
    first_half = (jnp.arange(HEAD_DIM) % (HEAD_DIM // 2)) < (HEAD_DIM // 4)
    sina = jnp.where(first_half[None], -sin, 0.0)
    sinb = jnp.where(first_half[None], 0.0, sin)
    return cos, sina, sinb


def _alibi_bias(group, dil):
    slopes = 2.0 ** (-ALIBI_MAX_BIAS * np.arange(1, A_HEADS + 1) / A_HEADS)
    slopes = slopes[group * A_HEADS_PER_GROUP:(group + 1) * A_HEADS_PER_GROUP]
    rel = np.arange(DIL_SUB + 2 * N_SIDE)[None, :] - N_SIDE - np.arange(DIL_SUB)[:, None]
    inside = np.abs(rel) <= N_SIDE
    bias = -slopes[:, None, None] * (dil * np.abs(rel)).astype(np.float32)[None]
    return jnp.asarray(np.where(inside[None], bias, MASK_VALUE), dtype=F32)


def _trunk(x, params, biases):
    batch, seq, _ = x.shape
    cos, sina, sinb = _rope_tables(seq)
    xf = x.reshape(batch * seq, D_MODEL)
    for layer in range(DEPTH):
        w_in, b_in, q_gain, k_gain, wpa, wpb, wo, ln_g, ln_b = (p[layer] for p in params)
        h = _in_proj(xf, w_in, b_in, q_gain, k_gain, cos, sina, sinb, seq)
        h3 = h.reshape(batch, seq, IN_WIDTH)
        yb = _gqa_flash(h3).reshape(batch * seq, B_HEADS * HEAD_DIM)
        outs, lses = [], []
        for group, (_, dil) in enumerate(A_GROUPS):
            o, lse = _dilated_group(h3, biases[group], group, dil)
            outs.append(o)
            lses.append(lse)
        xf = _post(xf, outs, lses, h, yb, wpa, wpb, wo, ln_g, ln_b)
    return xf.reshape(batch, seq, D_MODEL)


def kernel(x_prompt, x_sample, w_in, b_in, q_gain, k_gain, w_proj_a, w_proj_b, w_out, ln_g, ln_b):
    params = (
        w_in.astype(BF16),
        b_in.reshape(DEPTH, 1, IN_WIDTH),
        q_gain.reshape(DEPTH, 1, HEAD_DIM),
        k_gain.reshape(DEPTH, 1, HEAD_DIM),
        w_proj_a.astype(BF16),
        w_proj_b.astype(BF16),
        w_out.astype(BF16),
        ln_g.reshape(DEPTH, 1, D_MODEL),
        ln_b.reshape(DEPTH, 1, D_MODEL),
    )
    biases = [_alibi_bias(group, dil) for group, (_, dil) in enumerate(A_GROUPS)]
    return (_trunk(x_prompt, params, biases), _trunk(x_sample, params, biases))
```

```python
import functools
import math

import jax
import jax.numpy as jnp
import numpy as np
from jax import lax
from jax.experimental import pallas as pl
from jax.experimental.pallas import tpu as pltpu

F32 = jnp.float32
BF16 = jnp.bfloat16

D_MODEL = 1024
DEPTH = 2
HEAD_DIM = 128
GRID_W = 64
A_GROUPS = ((128, 1), (512, 4), (2048, 16))
A_HEADS_PER_GROUP = 4
A_HEADS = 12
A_WIDTH = 512
ALIBI_MAX_BIAS = 8.0
B_HEADS = 8
B_KV_HEADS = 2
B_GROUP = B_HEADS // B_KV_HEADS
ROPE_THETA = 10000.0
IN_WIDTH = 9728
RMS_EPS = 1e-6
LN_EPS = 1e-5
MASK_VALUE = -1e30
ALPHA = (2 * DEPTH) ** 0.25
QK_SCALE = 1.0 / math.sqrt(HEAD_DIM)
BQ_SCALE = QK_SCALE * math.log2(math.e)
ONES_ROWS = 16
LOGIT_BOUND = 60.0

COL_TILE = 512
N_COL_TILES = IN_WIDTH // COL_TILE
T_AQ, T_AK, T_AV, T_AG = 0, 3, 6, 9
T_BQ, T_BKV, T_BG = 10, 12, 13
T_GATE_A, T_GATE_B = 15, 17
BK_COL128 = 6144 // HEAD_DIM
BV_COL128 = 6400 // HEAD_DIM

N_SIDE = 64

VMEM_LIMIT_BYTES = 56 * 1024 * 1024

SUPER_TILE = 2048
IN_PROJ_CHUNK = 512
FLASH_Q_ROWS = 512
FLASH_K_ROWS = 512
DIL_SUB = 128
POST_ROWS = 512


def _sigmoid(x):
    return 0.5 * jnp.tanh(0.5 * x) + 0.5


def _in_proj_kernel(x_ref, w_ref, b_ref, qg_ref, kg_ref, cos_ref, sin_ref, o_ref, xb_ref, hs_ref):
    j = pl.program_id(1)
    tm = x_ref.shape[0]

    @pl.when(j == 0)
    def _():
        xb_ref[...] = x_ref[...].astype(BF16)

    lane_tiles = COL_TILE // HEAD_DIM
    n_chunks = tm // IN_PROJ_CHUNK

    def column_tile(epilogue):
        def chunk_rows(rc):
            return slice(rc * IN_PROJ_CHUNK, (rc + 1) * IN_PROJ_CHUNK)

        def matmul(rc):
            return jnp.dot(xb_ref[chunk_rows(rc), :], w_ref[...], preferred_element_type=F32) + b_ref[...]

        def run():
            h_next = matmul(0)
            for rc in range(n_chunks):
                h = h_next
                if rc + 1 < n_chunks:
                    h_next = matmul(rc + 1)
                epilogue(rc, chunk_rows(rc), h)
        return run

    def rms_rope(hh, gain, rows):
        ms = jnp.mean(hh * hh, axis=-1, keepdims=True)
        y = hh * lax.rsqrt(ms + RMS_EPS) * gain
        return y * cos_ref[rows, :] + pltpu.roll(y, HEAD_DIM // 2, 1) * sin_ref[rows, :]

    a_scale = jnp.where(j < T_AK, QK_SCALE, 1.0)
    a_group = j % len(A_GROUPS)

    def plain(rc, rows, h):
        o_ref[rows, :] = (h * a_scale).astype(BF16)

    def class_major(dil):
        n = tm // dil
        per_chunk = IN_PROJ_CHUNK // dil

        def epilogue(rc, rows, h):
            for ln in range(lane_tiles):
                hs_ref[rc, ln] = h[:, ln * HEAD_DIM:(ln + 1) * HEAD_DIM] * a_scale
            for c in range(dil):
                dst = slice(c * n + rc * per_chunk, c * n + (rc + 1) * per_chunk)
                for ln in range(lane_tiles):
                    o_ref[dst, ln * HEAD_DIM:(ln + 1) * HEAD_DIM] = (
                        hs_ref[rc, ln, pl.ds(c, per_chunk, stride=dil), :].astype(BF16))
        return epilogue

    def silu(rc, rows, h):
        o_ref[rows, :] = (h * _sigmoid(h)).astype(BF16)

    def rope_q(rc, rows, h):
        for hd in range(lane_tiles):
            sl = slice(hd * HEAD_DIM, (hd + 1) * HEAD_DIM)
            o_ref[rows, sl] = (rms_rope(h[:, sl], qg_ref[...], rows) * BQ_SCALE).astype(BF16)

    def rope_k_and_v(rc, rows, h):
        for hd in range(B_KV_HEADS):
            sl = slice(hd * HEAD_DIM, (hd + 1) * HEAD_DIM)
            o_ref[rows, sl] = rms_rope(h[:, sl], kg_ref[...], rows).astype(BF16)
        o_ref[rows, B_KV_HEADS * HEAD_DIM:] = h[:, B_KV_HEADS * HEAD_DIM:].astype(BF16)

    def gate(rc, rows, h):
        o_ref[rows, :] = _sigmoid(h).astype(BF16)

    for group, (_, dil) in enumerate(A_GROUPS):
        pl.when((j < T_AG) & (a_group == group))(column_tile(plain if dil == 1 else class_major(dil)))
    pl.when((j == T_AG) | (j == T_BG) | (j == T_BG + 1))(column_tile(silu))
    pl.when((j == T_BQ) | (j == T_BQ + 1))(column_tile(rope_q))
    pl.when(j == T_BKV)(column_tile(rope_k_and_v))
    pl.when(j >= T_GATE_A)(column_tile(gate))


def _in_proj(x, w, b, q_gain, k_gain, cos, sin, seq):
    tokens = x.shape[0]
    tm = SUPER_TILE
    assert seq % tm == 0
    pos_blocks = seq // tm
    row_spec = lambda width: pl.BlockSpec((tm, width), lambda i, j: (i, 0))
    pos_spec = pl.BlockSpec((tm, HEAD_DIM), lambda i, j: (i % pos_blocks, 0))
    vec_spec = pl.BlockSpec((1, HEAD_DIM), lambda i, j: (0, 0))
    return pl.pallas_call(
        _in_proj_kernel,
        grid=(tokens // tm, N_COL_TILES),
        in_specs=[
            row_spec(D_MODEL),
            pl.BlockSpec((D_MODEL, COL_TILE), lambda i, j: (0, j)),
            pl.BlockSpec((1, COL_TILE), lambda i, j: (0, j)),
            vec_spec, vec_spec, pos_spec, pos_spec,
        ],
        out_specs=pl.BlockSpec((tm, COL_TILE), lambda i, j: (i, j)),
        out_shape=jax.ShapeDtypeStruct((tokens, IN_WIDTH), BF16),
        scratch_shapes=[
            pltpu.VMEM((tm, D_MODEL), BF16),
            pltpu.VMEM((tm // IN_PROJ_CHUNK, COL_TILE // HEAD_DIM, IN_PROJ_CHUNK, HEAD_DIM), F32),
        ],
        compiler_params=pltpu.CompilerParams(
            dimension_semantics=("arbitrary", "arbitrary"), vmem_limit_bytes=VMEM_LIMIT_BYTES),
        name="in_proj",
    )(x, w, b, q_gain, k_gain, cos, sin)


def _flash_kernel(q_ref, k_ref, v_ref, g_ref, o_ref, qt_ref, vt_ref, m_ref, acc_ref, k2max_ref,
                  *, tq, tk, n_chunks):
    i = pl.program_id(2)

    @pl.when(i == 0)
    def _():
        def fill(c, k2max):
            start = pl.multiple_of(c * tk, tk)
            vt_ref[c, 0:HEAD_DIM, :] = v_ref[pl.ds(start, tk), :].astype(F32).T.astype(BF16)
            vt_ref[c, HEAD_DIM:, :] = jnp.ones((ONES_ROWS, tk), BF16)
            kf = k_ref[pl.ds(start, tk), :].astype(F32)
            k2 = jnp.max(jnp.sum(kf * kf, axis=-1, keepdims=True), axis=0, keepdims=True)
            return jnp.maximum(k2max, k2)

        k2max_ref[...] = lax.fori_loop(0, n_chunks, fill, jnp.zeros((1, 1), F32))

    q2 = jnp.zeros((1, 1), F32)
    for hd in range(B_GROUP):
        qf = q_ref[:, hd * HEAD_DIM:(hd + 1) * HEAD_DIM].astype(F32).T
        qt_ref[:, hd * tq:(hd + 1) * tq] = qf.astype(BF16)
        q2 = jnp.maximum(q2, jnp.max(jnp.sum(qf * qf, axis=0, keepdims=True), axis=1, keepdims=True))
    acc_ref[...] = jnp.zeros(acc_ref.shape, F32)

    logits_bounded = (q2 * k2max_ref[...])[0, 0] <= LOGIT_BOUND * LOGIT_BOUND

    def scores(c):
        start = pl.multiple_of(c * tk, tk)
        return jnp.dot(k_ref[pl.ds(start, tk), :], qt_ref[...], preferred_element_type=F32)

    def bounded_body(c, carry):
        pt = jnp.exp2(scores(c)).astype(BF16)
        acc_ref[...] += jnp.dot(vt_ref[c], pt, preferred_element_type=F32)
        return carry

    def online_body(c, carry):
        st = scores(c)
        m_prev = m_ref[...]
        m_new = jnp.maximum(m_prev, jnp.max(st, axis=0, keepdims=True))
        alpha = jnp.exp2(m_prev - m_new)
        pt = jnp.exp2(st - m_new).astype(BF16)
        acc_ref[...] = alpha * acc_ref[...] + jnp.dot(vt_ref[c], pt, preferred_element_type=F32)
        m_ref[...] = m_new
        return carry

    @pl.when(logits_bounded)
    def _():
        lax.fori_loop(0, n_chunks, bounded_body, 0, unroll=2)

    @pl.when(jnp.logical_not(logits_bounded))
    def _():
        m_ref[...] = jnp.full(m_ref.shape, MASK_VALUE, F32)
        lax.fori_loop(0, n_chunks, online_body, 0, unroll=2)

    acc = acc_ref[...]
    ot = acc[0:HEAD_DIM, :] / acc[HEAD_DIM:HEAD_DIM + 1, :]
    for hd in range(B_GROUP):
        sl = slice(hd * HEAD_DIM, (hd + 1) * HEAD_DIM)
        o_ref[:, sl] = (ot[:, hd * tq:(hd + 1) * tq].T * g_ref[:, sl].astype(F32)).astype(BF16)


def _gqa_flash(h3):
    batch, seq, _ = h3.shape
    tq, tk = FLASH_Q_ROWS, FLASH_K_ROWS
    n_chunks = seq // tk
    kernel = functools.partial(_flash_kernel, tq=tq, tk=tk, n_chunks=n_chunks)
    return pl.pallas_call(
        kernel,
        grid=(batch, B_KV_HEADS, seq // tq),
        in_specs=[
            pl.BlockSpec((None, tq, COL_TILE), lambda b, g, i: (b, i, T_BQ + g)),
            pl.BlockSpec((None, seq, HEAD_DIM), lambda b, g, i: (b, 0, BK_COL128 + g)),
            pl.BlockSpec((None, seq, HEAD_DIM), lambda b, g, i: (b, 0, BV_COL128 + g)),
            pl.BlockSpec((None, tq, COL_TILE), lambda b, g, i: (b, i, T_BG + g)),
        ],
        out_specs=pl.BlockSpec((None, tq, COL_TILE), lambda b, g, i: (b, i, g)),
        out_shape=jax.ShapeDtypeStruct((batch, seq, B_HEADS * HEAD_DIM), BF16),
        scratch_shapes=[
            pltpu.VMEM((HEAD_DIM, B_GROUP * tq), BF16),
            pltpu.VMEM((n_chunks, HEAD_DIM + ONES_ROWS, tk), BF16),
            pltpu.VMEM((1, B_GROUP * tq), F32),
            pltpu.VMEM((HEAD_DIM + ONES_ROWS, B_GROUP * tq), F32),
            pltpu.VMEM((1, 1), F32),
        ],
        compiler_params=pltpu.CompilerParams(
            dimension_semantics=("arbitrary", "arbitrary", "arbitrary"), vmem_limit_bytes=VMEM_LIMIT_BYTES),
        name="gqa_flash",
    )(h3, h3, h3, h3)


def _strided_rows(start, size, stride):
    return pl.ds(start, size) if stride == 1 else pl.ds(start, size, stride=stride)


def _aligned(index, multiple):
    return index if isinstance(index, int) else pl.multiple_of(index, multiple)


def _dilated_kernel(q_ref, k_ref, v_ref, kp_ref, kn_ref, vp_ref, vn_ref, bias_ref, o_ref, lse_ref,
                    kw_ref, vw_ref, stage_ref, *, dil, length):
    n = SUPER_TILE // dil
    win = DIL_SUB + 2 * N_SIDE
    first_pos = pl.program_id(1) * n
    lanes_per_head = HEAD_DIM // A_HEADS_PER_GROUP

    def sub_block(c, a):
        r0 = _aligned(a * DIL_SUB, DIL_SUB)
        key_pos = first_pos + a * DIL_SUB - N_SIDE + lax.broadcasted_iota(jnp.int32, (1, win), 1)
        key_ok = (key_pos >= 0) & (key_pos < length)
        token_rows = _strided_rows(c + dil * r0, DIL_SUB, dil)
        lse_cols = []
        for hd in range(A_HEADS_PER_GROUP):
            sl = slice(hd * HEAD_DIM, (hd + 1) * HEAD_DIM)
            q = q_ref[pl.ds(_aligned(c * n + r0, DIL_SUB), DIL_SUB), sl]
            k = kw_ref[c, pl.ds(r0, win), sl]
            v = vw_ref[c, pl.ds(r0, win), sl]
            s = lax.dot_general(q, k, (((1,), (1,)), ((), ())), preferred_element_type=F32) + bias_ref[hd]
            s = jnp.where(key_ok, s, MASK_VALUE)
            m = jnp.max(s, axis=-1, keepdims=True)
            p = jnp.exp(s - m)
            l = jnp.sum(p, axis=-1, keepdims=True)
            stage_ref[hd, token_rows, :] = jnp.dot(p.astype(BF16), v, preferred_element_type=F32) / l
            lse_cols.append(jnp.broadcast_to(m + jnp.log(l), (DIL_SUB, lanes_per_head)))
        lse_ref[token_rows, :] = jnp.concatenate(lse_cols, axis=-1)

    for c in range(dil):
        rows = slice(c * n, (c + 1) * n)
        kw_ref[c, 0:N_SIDE, :] = kp_ref[c]
        kw_ref[c, N_SIDE:N_SIDE + n, :] = k_ref[rows, :]
        kw_ref[c, N_SIDE + n:, :] = kn_ref[c]
        vw_ref[c, 0:N_SIDE, :] = vp_ref[c]
        vw_ref[c, N_SIDE:N_SIDE + n, :] = v_ref[rows, :]
        vw_ref[c, N_SIDE + n:, :] = vn_ref[c]

    subs_per_class = n // DIL_SUB

    def one_sub_block(sb, carry):
        sub_block(sb // subs_per_class, sb % subs_per_class)
        return carry

    lax.fori_loop(0, SUPER_TILE // DIL_SUB, one_sub_block, 0, unroll=2)

    for hd in range(A_HEADS_PER_GROUP):
        o_ref[:, hd * HEAD_DIM:(hd + 1) * HEAD_DIM] = stage_ref[hd].astype(BF16)


def _dilated_group(h3, bias, group, dil):
    batch, seq, _ = h3.shape
    tiles = seq // SUPER_TILE
    n = SUPER_TILE // dil
    halos = n // N_SIDE
    h6 = h3.reshape(batch, tiles, dil, halos, N_SIDE, IN_WIDTH)

    def cur(tile):
        return pl.BlockSpec((None, SUPER_TILE, COL_TILE), lambda b, t: (b, t, tile + group))

    def prev(tile):
        return pl.BlockSpec((None, None, dil, None, N_SIDE, COL_TILE),
                            lambda b, t: (b, jnp.maximum(t - 1, 0), 0, halos - 1, 0, tile + group))

    def nxt(tile):
        return pl.BlockSpec((None, None, dil, None, N_SIDE, COL_TILE),
                            lambda b, t: (b, jnp.minimum(t + 1, tiles - 1), 0, 0, 0, tile + group))

    kernel = functools.partial(_dilated_kernel, dil=dil, length=seq // dil)
    o, lse = pl.pallas_call(
        kernel,
        grid=(batch, tiles),
        in_specs=[
            cur(T_AQ), cur(T_AK), cur(T_AV), prev(T_AK), nxt(T_AK), prev(T_AV), nxt(T_AV),
            pl.BlockSpec((A_HEADS_PER_GROUP, DIL_SUB, DIL_SUB + 2 * N_SIDE), lambda b, t: (0, 0, 0)),
        ],
        out_specs=[
            pl.BlockSpec((None, SUPER_TILE, A_WIDTH), lambda b, t: (b, t, 0)),
            pl.BlockSpec((None, SUPER_TILE, HEAD_DIM), lambda b, t: (b, t, 0)),
        ],
        out_shape=[
            jax.ShapeDtypeStruct((batch, seq, A_WIDTH), BF16),
            jax.ShapeDtypeStruct((batch, seq, HEAD_DIM), F32),
        ],
        scratch_shapes=[
            pltpu.VMEM((dil, n + 2 * N_SIDE, COL_TILE), BF16),
            pltpu.VMEM((dil, n + 2 * N_SIDE, COL_TILE), BF16),
            pltpu.VMEM((A_HEADS_PER_GROUP, SUPER_TILE, HEAD_DIM), F32),
        ],
        compiler_params=pltpu.CompilerParams(
            dimension_semantics=("arbitrary", "arbitrary"), vmem_limit_bytes=VMEM_LIMIT_BYTES),
        name=f"dilated_g{group}",
    )(h3, h3, h3, h6, h6, h6, h6, bias)
    return o.reshape(batch * seq, A_WIDTH), lse.reshape(batch * seq, HEAD_DIM)


def _post_kernel(x_ref, o0_ref, o1_ref, o2_ref, l0_ref, l1_ref, l2_ref, ag_ref, yb_ref,
                 ga0_ref, ga1_ref, gb0_ref, gb1_ref, wpa_ref, wpb_ref, wo_ref, lng_ref, lnb_ref, y_ref):
    l0, l1, l2 = l0_ref[...], l1_ref[...], l2_ref[...]
    top = jnp.maximum(jnp.maximum(l0, l1), l2)
    e0, e1, e2 = jnp.exp(l0 - top), jnp.exp(l1 - top), jnp.exp(l2 - top)
    inv = 1.0 / (e0 + e1 + e2)
    lanes_per_head = HEAD_DIM // A_HEADS_PER_GROUP
    rows = x_ref.shape[0]
    ya_heads = []
    for hd in range(A_HEADS_PER_GROUP):
        sl = slice(hd * HEAD_DIM, (hd + 1) * HEAD_DIM)
        pick = slice(hd * lanes_per_head, hd * lanes_per_head + 1)
        w0 = jnp.broadcast_to((e0 * inv)[:, pick], (rows, HEAD_DIM))
        w1 = jnp.broadcast_to((e1 * inv)[:, pick], (rows, HEAD_DIM))
        w2 = jnp.broadcast_to((e2 * inv)[:, pick], (rows, HEAD_DIM))
        mix = (w0 * o0_ref[:, sl].astype(F32) + w1 * o1_ref[:, sl].astype(F32)
               + w2 * o2_ref[:, sl].astype(F32))
        ya_heads.append((mix * ag_ref[:, sl].astype(F32)).astype(BF16))
    ya = jnp.concatenate(ya_heads, axis=-1)
    pa = jnp.dot(ya, wpa_ref[...], preferred_element_type=F32)
    pb = jnp.dot(yb_ref[...], wpb_ref[...], preferred_element_type=F32)
    gate_a = jnp.concatenate([ga0_ref[...], ga1_ref[...]], axis=-1).astype(F32)
    gate_b = jnp.concatenate([gb0_ref[...], gb1_ref[...]], axis=-1).astype(F32)
    merged = (gate_a * pa + gate_b * pb).astype(BF16)
    z = ALPHA * x_ref[...] + jnp.dot(merged, wo_ref[...], preferred_element_type=F32)
    mu = jnp.mean(z, axis=-1, keepdims=True)
    zc = z - mu
    var = jnp.mean(zc * zc, axis=-1, keepdims=True)
    y_ref[...] = zc * lax.rsqrt(var + LN_EPS) * lng_ref[...] + lnb_ref[...]


def _post(x, outs, lses, h, yb, wpa, wpb, wo, ln_g, ln_b):
    tokens = x.shape[0]
    tm = POST_ROWS
    row = lambda width, col=0: pl.BlockSpec((tm, width), lambda i, col=col: (i, col))
    full = lambda a: pl.BlockSpec(a.shape, lambda i: (0, 0))
    return pl.pallas_call(
        _post_kernel,
        grid=(tokens // tm,),
        in_specs=[
            row(D_MODEL),
            row(A_WIDTH), row(A_WIDTH), row(A_WIDTH),
            row(HEAD_DIM), row(HEAD_DIM), row(HEAD_DIM),
            row(COL_TILE, T_AG),
            row(B_HEADS * HEAD_DIM),
            row(COL_TILE, T_GATE_A), row(COL_TILE, T_GATE_A + 1),
            row(COL_TILE, T_GATE_B), row(COL_TILE, T_GATE_B + 1),
            full(wpa), full(wpb), full(wo), full(ln_g), full(ln_b),
        ],
        out_specs=row(D_MODEL),
        out_shape=jax.ShapeDtypeStruct((tokens, D_MODEL), F32),
        compiler_params=pltpu.CompilerParams(
            dimension_semantics=("arbitrary",), vmem_limit_bytes=VMEM_LIMIT_BYTES),
        name="post",
    )(x, *outs, *lses, h, yb, h, h, h, h, wpa, wpb, wo, ln_g, ln_b)


def _rope_tables(seq):
    rows = seq // GRID_W
    row_pos = jnp.repeat(jnp.arange(rows), GRID_W).astype(F32)
    col_pos = jnp.tile(jnp.arange(GRID_W), rows).astype(F32)
    axis_dim = HEAD_DIM // 2
    inv_freq = ROPE_THETA ** (-jnp.arange(0, axis_dim, 2, dtype=F32) / axis_dim)
    ang_r = row_pos[:, None] * inv_freq[None]
    ang_c = col_pos[:, None] * inv_freq[None]
    cos = jnp.cos(jnp.concatenate([ang_r, ang_c, ang_r, ang_c], axis=-1))
    sin_r, sin_c = jnp.sin(ang_r), jnp.sin(ang_c)
    return cos, jnp.concatenate([-sin_r, -sin_c, sin_r, sin_c], axis=-1)


def _pair_rotary_halves(a, first_col, n_heads):
    seg = a[..., first_col:first_col + n_heads * HEAD_DIM]
    seg = seg.reshape(*seg.shape[:-1], n_heads, 2, 2, HEAD_DIM // 4)
    seg = jnp.swapaxes(seg, -3, -2).reshape(*a.shape[:-1], n_heads * HEAD_DIM)
    return jnp.concatenate([a[..., :first_col], seg, a[..., first_col + n_heads * HEAD_DIM:]], axis=-1)


def _alibi_bias(group, dil):
    slopes = 2.0 ** (-ALIBI_MAX_BIAS * np.arange(1, A_HEADS + 1) / A_HEADS)
    slopes = slopes[group * A_HEADS_PER_GROUP:(group + 1) * A_HEADS_PER_GROUP]
    rel = np.arange(DIL_SUB + 2 * N_SIDE)[None, :] - N_SIDE - np.arange(DIL_SUB)[:, None]
    inside = np.abs(rel) <= N_SIDE
    bias = -slopes[:, None, None] * (dil * np.abs(rel)).astype(np.float32)[None]
    return jnp.asarray(np.where(inside[None], bias, MASK_VALUE), dtype=F32)


def _trunk(x, params, biases):
    batch, seq, _ = x.shape
    cos, sin = _rope_tables(seq)
    xf = x.reshape(batch * seq, D_MODEL)
    for layer in range(DEPTH):
        w_in, b_in, q_gain, k_gain, wpa, wpb, wo, ln_g, ln_b = (p[layer] for p in params)
        h = _in_proj(xf, w_in, b_in, q_gain, k_gain, cos, sin, seq)
        h3 = h.reshape(batch, seq, IN_WIDTH)
        yb = _gqa_flash(h3).reshape(batch * seq, B_HEADS * HEAD_DIM)
        outs, lses = [], []
        for group, (_, dil) in enumerate(A_GROUPS):
            o, lse = _dilated_group(h3, biases[group], group, dil)
            outs.append(o)
            lses.append(lse)
        xf = _post(xf, outs, lses, h, yb, wpa, wpb, wo, ln_g, ln_b)
    return xf.reshape(batch, seq, D_MODEL)


def _prepare_params(w_in, b_in, q_gain, k_gain, w_proj_a, w_proj_b, w_out, ln_g, ln_b):
    rotary_heads = B_HEADS + B_KV_HEADS
    first_rotary_col = T_BQ * COL_TILE
    return (
        _pair_rotary_halves(w_in, first_rotary_col, rotary_heads).astype(BF16),
        _pair_rotary_halves(b_in, first_rotary_col, rotary_heads).reshape(DEPTH, 1, IN_WIDTH),
        _pair_rotary_halves(q_gain, 0, 1).reshape(DEPTH, 1, HEAD_DIM),
        _pair_rotary_halves(k_gain, 0, 1).reshape(DEPTH, 1, HEAD_DIM),
        w_proj_a.astype(BF16),
        w_proj_b.astype(BF16),
        w_out.astype(BF16),
        ln_g.reshape(DEPTH, 1, D_MODEL),
        ln_b.reshape(DEPTH, 1, D_MODEL),
    )


def kernel(x_prompt, x_sample, w_in, b_in, q_gain, k_gain, w_proj_a, w_proj_b, w_out, ln_g, ln_b):
    params = _prepare_params(w_in, b_in, q_gain, k_gain, w_proj_a, w_proj_b, w_out, ln_g, ln_b)
    biases = [_alibi_bias(group, dil) for group, (_, dil) in enumerate(A_GROUPS)]
    return (_trunk(x_prompt, params, biases), _trunk(x_sample, params, biases))
```

```python
import functools
import math

import jax
import jax.numpy as jnp
import numpy as np
from jax import lax
from jax.experimental import pallas as pl
from jax.experimental.pallas import tpu as pltpu

F32 = jnp.float32
BF16 = jnp.bfloat16

D_MODEL = 1024
DEPTH = 2
HEAD_DIM = 128
GRID_W = 64
A_GROUPS = ((128, 1), (512, 4), (2048, 16))
A_HEADS_PER_GROUP = 4
A_HEADS = 12
A_WIDTH = 512
ALIBI_MAX_BIAS = 8.0
B_HEADS = 8
B_KV_HEADS = 2
B_GROUP = B_HEADS // B_KV_HEADS
ROPE_THETA = 10000.0
IN_WIDTH = 9728
RMS_EPS = 1e-6
LN_EPS = 1e-5
MASK_VALUE = -1e30
ALPHA = (2 * DEPTH) ** 0.25
QK_SCALE = 1.0 / math.sqrt(HEAD_DIM)
BQ_SCALE = QK_SCALE * math.log2(math.e)
ONES_ROWS = 16
LOGIT_BOUND = 60.0

COL_TILE = 512
N_COL_TILES = IN_WIDTH // COL_TILE
T_AQ, T_AK, T_AV, T_AG = 0, 3, 6, 9
T_BQ, T_BKV, T_BG = 10, 12, 13
T_GATE_A, T_GATE_B = 15, 17
BK_COL128 = T_BKV * COL_TILE // HEAD_DIM
BV_COL128 = BK_COL128 + B_KV_HEADS

N_SIDE = 64

VMEM_LIMIT_BYTES = 56 * 1024 * 1024

SUPER_TILE = 2048
IN_PROJ_CHUNK = 512
FLASH_Q_ROWS = 1024
FLASH_K_ROWS = 512
DIL_SUB = 128
DIL_UNROLL = 4
POST_ROWS = 512
POST_CHUNK = 256


def _sigmoid(x):
    return 0.5 * jnp.tanh(0.5 * x) + 0.5


def _in_proj_kernel(x_ref, w_ref, b_ref, qg_ref, kg_ref, cos_ref, sin_ref, o_ref, xb_ref, hs_ref):
    j = pl.program_id(1)
    tm = x_ref.shape[0]

    @pl.when(j == 0)
    def _():
        xb_ref[...] = x_ref[...].astype(BF16)

    lane_tiles = COL_TILE // HEAD_DIM
    n_chunks = tm // IN_PROJ_CHUNK

    def column_tile(epilogue):
        def run():
            for rc in range(n_chunks):
                rows = slice(rc * IN_PROJ_CHUNK, (rc + 1) * IN_PROJ_CHUNK)
                h = jnp.dot(xb_ref[rows, :], w_ref[...], preferred_element_type=F32) + b_ref[...]
                epilogue(rc, rows, h)
        return run

    def rms_rope(hh, gain, rows):
        ms = jnp.mean(hh * hh, axis=-1, keepdims=True)
        y = hh * lax.rsqrt(ms + RMS_EPS) * gain
        return y * cos_ref[rows, :] + pltpu.roll(y, HEAD_DIM // 2, 1) * sin_ref[rows, :]

    a_scale = jnp.where(j < T_AK, QK_SCALE, 1.0)
    a_group = j % len(A_GROUPS)

    def plain(rc, rows, h):
        o_ref[rows, :] = (h * a_scale).astype(BF16)

    def class_major(dil):
        n = tm // dil
        per_chunk = IN_PROJ_CHUNK // dil

        def epilogue(rc, rows, h):
            for ln in range(lane_tiles):
                hs_ref[rc, ln] = h[:, ln * HEAD_DIM:(ln + 1) * HEAD_DIM] * a_scale
            for c in range(dil):
                dst = slice(c * n + rc * per_chunk, c * n + (rc + 1) * per_chunk)
                for ln in range(lane_tiles):
                    o_ref[dst, ln * HEAD_DIM:(ln + 1) * HEAD_DIM] = (
                        hs_ref[rc, ln, pl.ds(c, per_chunk, stride=dil), :].astype(BF16))
        return epilogue

    def silu(rc, rows, h):
        o_ref[rows, :] = (h * _sigmoid(h)).astype(BF16)

    def rope_q(rc, rows, h):
        for hd in range(lane_tiles):
            sl = slice(hd * HEAD_DIM, (hd + 1) * HEAD_DIM)
            o_ref[rows, sl] = (rms_rope(h[:, sl], qg_ref[...], rows) * BQ_SCALE).astype(BF16)

    def rope_k_and_v(rc, rows, h):
        for hd in range(B_KV_HEADS):
            sl = slice(hd * HEAD_DIM, (hd + 1) * HEAD_DIM)
            o_ref[rows, sl] = rms_rope(h[:, sl], kg_ref[...], rows).astype(BF16)
        o_ref[rows, B_KV_HEADS * HEAD_DIM:] = h[:, B_KV_HEADS * HEAD_DIM:].astype(BF16)

    def gate(rc, rows, h):
        o_ref[rows, :] = _sigmoid(h).astype(BF16)

    for group, (_, dil) in enumerate(A_GROUPS):
        pl.when((j < T_AG) & (a_group == group))(column_tile(plain if dil == 1 else class_major(dil)))
    pl.when((j == T_AG) | (j == T_BG) | (j == T_BG + 1))(column_tile(silu))
    pl.when((j == T_BQ) | (j == T_BQ + 1))(column_tile(rope_q))
    pl.when(j == T_BKV)(column_tile(rope_k_and_v))
    pl.when(j >= T_GATE_A)(column_tile(gate))


def _in_proj(x, w, b, q_gain, k_gain, cos, sin, seq):
    tokens = x.shape[0]
    tm = SUPER_TILE
    assert seq % tm == 0
    pos_blocks = seq // tm
    row_spec = lambda width: pl.BlockSpec((tm, width), lambda i, j: (i, 0))
    pos_spec = pl.BlockSpec((tm, HEAD_DIM), lambda i, j: (i % pos_blocks, 0))
    vec_spec = pl.BlockSpec((1, HEAD_DIM), lambda i, j: (0, 0))
    return pl.pallas_call(
        _in_proj_kernel,
        grid=(tokens // tm, N_COL_TILES),
        in_specs=[
            row_spec(D_MODEL),
            pl.BlockSpec((None, D_MODEL, COL_TILE), lambda i, j: (j, 0, 0)),
            pl.BlockSpec((1, COL_TILE), lambda i, j: (0, j)),
            vec_spec, vec_spec, pos_spec, pos_spec,
        ],
        out_specs=pl.BlockSpec((tm, COL_TILE), lambda i, j: (i, j)),
        out_shape=jax.ShapeDtypeStruct((tokens, IN_WIDTH), BF16),
        scratch_shapes=[
            pltpu.VMEM((tm, D_MODEL), BF16),
            pltpu.VMEM((tm // IN_PROJ_CHUNK, COL_TILE // HEAD_DIM, IN_PROJ_CHUNK, HEAD_DIM), F32),
        ],
        compiler_params=pltpu.CompilerParams(
            dimension_semantics=("arbitrary", "arbitrary"), vmem_limit_bytes=VMEM_LIMIT_BYTES),
        name="in_proj",
    )(x, w, b, q_gain, k_gain, cos, sin)


def _flash_kernel(q_ref, k_ref, v_ref, g_ref, o_ref, qt_ref, vt_ref, m_ref, acc_ref, k2max_ref,
                  *, tq, tk, n_chunks):
    i = pl.program_id(2)

    @pl.when(i == 0)
    def _():
        def fill(c, k2max):
            start = pl.multiple_of(c * tk, tk)
            vt_ref[c, 0:HEAD_DIM, :] = v_ref[pl.ds(start, tk), :].astype(F32).T.astype(BF16)
            vt_ref[c, HEAD_DIM:, :] = jnp.ones((ONES_ROWS, tk), BF16)
            kf = k_ref[pl.ds(start, tk), :].astype(F32)
            k2 = jnp.max(jnp.sum(kf * kf, axis=-1, keepdims=True), axis=0, keepdims=True)
            return jnp.maximum(k2max, k2)

        k2max_ref[...] = lax.fori_loop(0, n_chunks, fill, jnp.zeros((1, 1), F32))

    q2 = jnp.zeros((1, 1), F32)
    for hd in range(B_GROUP):
        qf = q_ref[:, hd * HEAD_DIM:(hd + 1) * HEAD_DIM].astype(F32).T
        qt_ref[:, hd * tq:(hd + 1) * tq] = qf.astype(BF16)
        q2 = jnp.maximum(q2, jnp.max(jnp.sum(qf * qf, axis=0, keepdims=True), axis=1, keepdims=True))
    acc_ref[...] = jnp.zeros(acc_ref.shape, F32)

    logits_bounded = (q2 * k2max_ref[...])[0, 0] <= LOGIT_BOUND * LOGIT_BOUND

    def scores(c):
        start = pl.multiple_of(c * tk, tk)
        return jnp.dot(k_ref[pl.ds(start, tk), :], qt_ref[...], preferred_element_type=F32)

    def bounded_body(c, carry):
        pt = jnp.exp2(scores(c)).astype(BF16)
        acc_ref[...] += jnp.dot(vt_ref[c], pt, preferred_element_type=F32)
        return carry

    def online_body(c, carry):
        st = scores(c)
        m_prev = m_ref[...]
        m_new = jnp.maximum(m_prev, jnp.max(st, axis=0, keepdims=True))
        alpha = jnp.exp2(m_prev - m_new)
        pt = jnp.exp2(st - m_new).astype(BF16)
        acc_ref[...] = alpha * acc_ref[...] + jnp.dot(vt_ref[c], pt, preferred_element_type=F32)
        m_ref[...] = m_new
        return carry

    @pl.when(logits_bounded)
    def _():
        lax.fori_loop(0, n_chunks, bounded_body, 0, unroll=2)

    @pl.when(jnp.logical_not(logits_bounded))
    def _():
        m_ref[...] = jnp.full(m_ref.shape, MASK_VALUE, F32)
        lax.fori_loop(0, n_chunks, online_body, 0, unroll=2)

    acc = acc_ref[...]
    ot = acc[0:HEAD_DIM, :] / acc[HEAD_DIM:HEAD_DIM + 1, :]
    for hd in range(B_GROUP):
        sl = slice(hd * HEAD_DIM, (hd + 1) * HEAD_DIM)
        o_ref[:, sl] = (ot[:, hd * tq:(hd + 1) * tq].T * g_ref[:, sl].astype(F32)).astype(BF16)


def _gqa_flash(h3):
    batch, seq, _ = h3.shape
    tq, tk = FLASH_Q_ROWS, FLASH_K_ROWS
    n_chunks = seq // tk
    kernel = functools.partial(_flash_kernel, tq=tq, tk=tk, n_chunks=n_chunks)
    return pl.pallas_call(
        kernel,
        grid=(batch, B_KV_HEADS, seq // tq),
        in_specs=[
            pl.BlockSpec((None, tq, COL_TILE), lambda b, g, i: (b, i, T_BQ + g)),
            pl.BlockSpec((None, seq, HEAD_DIM), lambda b, g, i: (b, 0, BK_COL128 + g)),
            pl.BlockSpec((None, seq, HEAD_DIM), lambda b, g, i: (b, 0, BV_COL128 + g)),
            pl.BlockSpec((None, tq, COL_TILE), lambda b, g, i: (b, i, T_BG + g)),
        ],
        out_specs=pl.BlockSpec((None, tq, COL_TILE), lambda b, g, i: (b, i, g)),
        out_shape=jax.ShapeDtypeStruct((batch, seq, B_HEADS * HEAD_DIM), BF16),
        scratch_shapes=[
            pltpu.VMEM((HEAD_DIM, B_GROUP * tq), BF16),
            pltpu.VMEM((n_chunks, HEAD_DIM + ONES_ROWS, tk), BF16),
            pltpu.VMEM((1, B_GROUP * tq), F32),
            pltpu.VMEM((HEAD_DIM + ONES_ROWS, B_GROUP * tq), F32),
            pltpu.VMEM((1, 1), F32),
        ],
        compiler_params=pltpu.CompilerParams(
            dimension_semantics=("arbitrary", "arbitrary", "arbitrary"), vmem_limit_bytes=VMEM_LIMIT_BYTES),
        name="gqa_flash",
    )(h3, h3, h3, h3)


def _strided_rows(start, size, stride):
    return pl.ds(start, size) if stride == 1 else pl.ds(start, size, stride=stride)


def _dilated_kernel(q_ref, k_ref, v_ref, kp_ref, kn_ref, vp_ref, vn_ref, bias_ref, o_ref, lse_ref,
                    kw_ref, vw_ref, stage_ref, *, dil, length):
    n = SUPER_TILE // dil
    win = DIL_SUB + 2 * N_SIDE
    first_pos = pl.program_id(1) * n
    lanes_per_head = HEAD_DIM // A_HEADS_PER_GROUP

    def sub_block(c, a):
        r0 = pl.multiple_of(a * DIL_SUB, DIL_SUB)
        key_pos = first_pos + a * DIL_SUB - N_SIDE + lax.broadcasted_iota(jnp.int32, (1, win), 1)
        key_ok = (key_pos >= 0) & (key_pos < length)
        token_rows = _strided_rows(c + dil * r0, DIL_SUB, dil)
        lse_cols = []
        for hd in range(A_HEADS_PER_GROUP):
            sl = slice(hd * HEAD_DIM, (hd + 1) * HEAD_DIM)
            q = q_ref[pl.ds(pl.multiple_of(c * n + r0, DIL_SUB), DIL_SUB), sl]
            k = kw_ref[c, pl.ds(r0, win), sl]
            v = vw_ref[c, pl.ds(r0, win), sl]
            s = lax.dot_general(q, k, (((1,), (1,)), ((), ())), preferred_element_type=F32) + bias_ref[hd]
            s = jnp.where(key_ok, s, MASK_VALUE)
            m = jnp.max(s, axis=-1, keepdims=True)
            p = jnp.exp(s - m)
            l = jnp.sum(p, axis=-1, keepdims=True)
            stage_ref[hd, token_rows, :] = jnp.dot(p.astype(BF16), v, preferred_element_type=F32) / l
            lse_cols.append(jnp.broadcast_to(m + jnp.log(l), (DIL_SUB, lanes_per_head)))
        lse_ref[token_rows, :] = jnp.concatenate(lse_cols, axis=-1)

    for c in range(dil):
        rows = slice(c * n, (c + 1) * n)
        kw_ref[c, 0:N_SIDE, :] = kp_ref[c]
        kw_ref[c, N_SIDE:N_SIDE + n, :] = k_ref[rows, :]
        kw_ref[c, N_SIDE + n:, :] = kn_ref[c]
        vw_ref[c, 0:N_SIDE, :] = vp_ref[c]
        vw_ref[c, N_SIDE:N_SIDE + n, :] = v_ref[rows, :]
        vw_ref[c, N_SIDE + n:, :] = vn_ref[c]

    subs_per_class = n // DIL_SUB

    def one_sub_block(sb, carry):
        sub_block(sb // subs_per_class, sb % subs_per_class)
        return carry

    lax.fori_loop(0, SUPER_TILE // DIL_SUB, one_sub_block, 0, unroll=DIL_UNROLL)

    for hd in range(A_HEADS_PER_GROUP):
        o_ref[:, hd * HEAD_DIM:(hd + 1) * HEAD_DIM] = stage_ref[hd].astype(BF16)


def _dilated_group(h3, bias, group, dil):
    batch, seq, _ = h3.shape
    tiles = seq // SUPER_TILE
    n = SUPER_TILE // dil
    halos = n // N_SIDE
    h6 = h3.reshape(batch, tiles, dil, halos, N_SIDE, IN_WIDTH)

    def cur(tile):
        return pl.BlockSpec((None, SUPER_TILE, COL_TILE), lambda b, t: (b, t, tile + group))

    def prev(tile):
        return pl.BlockSpec((None, None, dil, None, N_SIDE, COL_TILE),
                            lambda b, t: (b, jnp.maximum(t - 1, 0), 0, halos - 1, 0, tile + group))

    def nxt(tile):
        return pl.BlockSpec((None, None, dil, None, N_SIDE, COL_TILE),
                            lambda b, t: (b, jnp.minimum(t + 1, tiles - 1), 0, 0, 0, tile + group))

    kernel = functools.partial(_dilated_kernel, dil=dil, length=seq // dil)
    o, lse = pl.pallas_call(
        kernel,
        grid=(batch, tiles),
        in_specs=[
            cur(T_AQ), cur(T_AK), cur(T_AV), prev(T_AK), nxt(T_AK), prev(T_AV), nxt(T_AV),
            pl.BlockSpec((A_HEADS_PER_GROUP, DIL_SUB, DIL_SUB + 2 * N_SIDE), lambda b, t: (0, 0, 0)),
        ],
        out_specs=[
            pl.BlockSpec((None, SUPER_TILE, A_WIDTH), lambda b, t: (b, t, 0)),
            pl.BlockSpec((None, SUPER_TILE, HEAD_DIM), lambda b, t: (b, t, 0)),
        ],
        out_shape=[
            jax.ShapeDtypeStruct((batch, seq, A_WIDTH), BF16),
            jax.ShapeDtypeStruct((batch, seq, HEAD_DIM), F32),
        ],
        scratch_shapes=[
            pltpu.VMEM((dil, n + 2 * N_SIDE, COL_TILE), BF16),
            pltpu.VMEM((dil, n + 2 * N_SIDE, COL_TILE), BF16),
            pltpu.VMEM((A_HEADS_PER_GROUP, SUPER_TILE, HEAD_DIM), F32),
        ],
        compiler_params=pltpu.CompilerParams(
            dimension_semantics=("arbitrary", "arbitrary"), vmem_limit_bytes=VMEM_LIMIT_BYTES),
        name=f"dilated_g{group}",
    )(h3, h3, h3, h6, h6, h6, h6, bias)
    return o.reshape(batch * seq, A_WIDTH), lse.reshape(batch * seq, HEAD_DIM)


def _post_kernel(x_ref, o0_ref, o1_ref, o2_ref, l0_ref, l1_ref, l2_ref, ag_ref, yb_ref,
                 ga0_ref, ga1_ref, gb0_ref, gb1_ref, wpa_ref, wpb_ref, wo_ref, lng_ref, lnb_ref, y_ref):
    lanes_per_head = HEAD_DIM // A_HEADS_PER_GROUP
    for rc in range(x_ref.shape[0] // POST_CHUNK):
        rows = slice(rc * POST_CHUNK, (rc + 1) * POST_CHUNK)
        l0, l1, l2 = l0_ref[rows, :], l1_ref[rows, :], l2_ref[rows, :]
        top = jnp.maximum(jnp.maximum(l0, l1), l2)
        e0, e1, e2 = jnp.exp(l0 - top), jnp.exp(l1 - top), jnp.exp(l2 - top)
        inv = 1.0 / (e0 + e1 + e2)
        ya_heads = []
        for hd in range(A_HEADS_PER_GROUP):
            sl = slice(hd * HEAD_DIM, (hd + 1) * HEAD_DIM)
            pick = slice(hd * lanes_per_head, hd * lanes_per_head + 1)
            w0 = jnp.broadcast_to((e0 * inv)[:, pick], (POST_CHUNK, HEAD_DIM))
            w1 = jnp.broadcast_to((e1 * inv)[:, pick], (POST_CHUNK, HEAD_DIM))
            w2 = jnp.broadcast_to((e2 * inv)[:, pick], (POST_CHUNK, HEAD_DIM))
            mix = (w0 * o0_ref[rows, sl].astype(F32) + w1 * o1_ref[rows, sl].astype(F32)
                   + w2 * o2_ref[rows, sl].astype(F32))
            ya_heads.append((mix * ag_ref[rows, sl].astype(F32)).astype(BF16))
        ya = jnp.concatenate(ya_heads, axis=-1)
        pa = jnp.dot(ya, wpa_ref[...], preferred_element_type=F32)
        pb = jnp.dot(yb_ref[rows, :], wpb_ref[...], preferred_element_type=F32)
        gate_a = jnp.concatenate([ga0_ref[rows, :], ga1_ref[rows, :]], axis=-1).astype(F32)
        gate_b = jnp.concatenate([gb0_ref[rows, :], gb1_ref[rows, :]], axis=-1).astype(F32)
        merged = (gate_a * pa + gate_b * pb).astype(BF16)
        z = ALPHA * x_ref[rows, :] + jnp.dot(merged, wo_ref[...], preferred_element_type=F32)
        mu = jnp.mean(z, axis=-1, keepdims=True)
        zc = z - mu
        var = jnp.mean(zc * zc, axis=-1, keepdims=True)
        y_ref[rows, :] = zc * lax.rsqrt(var + LN_EPS) * lng_ref[...] + lnb_ref[...]


def _post(x, outs, lses, h, yb, wpa, wpb, wo, ln_g, ln_b):
    tokens = x.shape[0]
    tm = POST_ROWS
    row = lambda width, col=0: pl.BlockSpec((tm, width), lambda i, col=col: (i, col))
    full = lambda a: pl.BlockSpec(a.shape, lambda i: (0, 0))
    return pl.pallas_call(
        _post_kernel,
        grid=(tokens // tm,),
        in_specs=[
            row(D_MODEL),
            row(A_WIDTH), row(A_WIDTH), row(A_WIDTH),
            row(HEAD_DIM), row(HEAD_DIM), row(HEAD_DIM),
            row(COL_TILE, T_AG),
            row(B_HEADS * HEAD_DIM),
            row(COL_TILE, T_GATE_A), row(COL_TILE, T_GATE_A + 1),
            row(COL_TILE, T_GATE_B), row(COL_TILE, T_GATE_B + 1),
            full(wpa), full(wpb), full(wo), full(ln_g), full(ln_b),
        ],
        out_specs=row(D_MODEL),
        out_shape=jax.ShapeDtypeStruct((tokens, D_MODEL), F32),
        compiler_params=pltpu.CompilerParams(
            dimension_semantics=("arbitrary",), vmem_limit_bytes=VMEM_LIMIT_BYTES),
        name="post",
    )(x, *outs, *lses, h, yb, h, h, h, h, wpa, wpb, wo, ln_g, ln_b)


def _rope_tables(seq):
    rows = seq // GRID_W
    row_pos = jnp.repeat(jnp.arange(rows), GRID_W).astype(F32)
    col_pos = jnp.tile(jnp.arange(GRID_W), rows).astype(F32)
    axis_dim = HEAD_DIM // 2
    inv_freq = ROPE_THETA ** (-jnp.arange(0, axis_dim, 2, dtype=F32) / axis_dim)
    ang_r = row_pos[:, None] * inv_freq[None]
    ang_c = col_pos[:, None] * inv_freq[None]
    cos = jnp.cos(jnp.concatenate([ang_r, ang_c, ang_r, ang_c], axis=-1))
    sin_r, sin_c = jnp.sin(ang_r), jnp.sin(ang_c)
    return cos, jnp.concatenate([-sin_r, -sin_c, sin_r, sin_c], axis=-1)


def _pair_rotary_halves(a, first_col, n_heads):
    seg = a[..., first_col:first_col + n_heads * HEAD_DIM]
    seg = seg.reshape(*seg.shape[:-1], n_heads, 2, 2, HEAD_DIM // 4)
    seg = jnp.swapaxes(seg, -3, -2).reshape(*a.shape[:-1], n_heads * HEAD_DIM)
    return jnp.concatenate([a[..., :first_col], seg, a[..., first_col + n_heads * HEAD_DIM:]], axis=-1)


def _alibi_bias(group, dil):
    slopes = 2.0 ** (-ALIBI_MAX_BIAS * np.arange(1, A_HEADS + 1) / A_HEADS)
    slopes = slopes[group * A_HEADS_PER_GROUP:(group + 1) * A_HEADS_PER_GROUP]
    rel = np.arange(DIL_SUB + 2 * N_SIDE)[None, :] - N_SIDE - np.arange(DIL_SUB)[:, None]
    inside = np.abs(rel) <= N_SIDE
    bias = -slopes[:, None, None] * (dil * np.abs(rel)).astype(np.float32)[None]
    return jnp.asarray(np.where(inside[None], bias, MASK_VALUE), dtype=F32)


def _trunk(x, params, biases):
    batch, seq, _ = x.shape
    cos, sin = _rope_tables(seq)
    xf = x.reshape(batch * seq, D_MODEL)
    for layer in range(DEPTH):
        w_in, b_in, q_gain, k_gain, wpa, wpb, wo, ln_g, ln_b = (p[layer] for p in params)
        h = _in_proj(xf, w_in, b_in, q_gain, k_gain, cos, sin, seq)
        h3 = h.reshape(batch, seq, IN_WIDTH)
        yb = _gqa_flash(h3).reshape(batch * seq, B_HEADS * HEAD_DIM)
        outs, lses = [], []
        for group, (_, dil) in enumerate(A_GROUPS):
            o, lse = _dilated_group(h3, biases[group], group, dil)
            outs.append(o)
            lses.append(lse)
        xf = _post(xf, outs, lses, h, yb, wpa, wpb, wo, ln_g, ln_b)
    return xf.reshape(batch, seq, D_MODEL)


def _prepare_params(w_in, b_in, q_gain, k_gain, w_proj_a, w_proj_b, w_out, ln_g, ln_b):
    rotary_heads = B_HEADS + B_KV_HEADS
    first_rotary_col = T_BQ * COL_TILE
    w_in = _pair_rotary_halves(w_in, first_rotary_col, rotary_heads).astype(BF16)
    return (
        w_in.reshape(DEPTH, D_MODEL, N_COL_TILES, COL_TILE).transpose(0, 2, 1, 3),
        _pair_rotary_halves(b_in, first_rotary_col, rotary_heads).reshape(DEPTH, 1, IN_WIDTH),
        _pair_rotary_halves(q_gain, 0, 1).reshape(DEPTH, 1, HEAD_DIM),
        _pair_rotary_halves(k_gain, 0, 1).reshape(DEPTH, 1, HEAD_DIM),
        w_proj_a.astype(BF16),
        w_proj_b.astype(BF16),
        w_out.astype(BF16),
        ln_g.reshape(DEPTH, 1, D_MODEL),
        ln_b.reshape(DEPTH, 1, D_MODEL),
    )


def kernel(x_prompt, x_sample, w_in, b_in, q_gain, k_gain, w_proj_a, w_proj_b, w_out, ln_g, ln_b):
    params = _prepare_params(w_in, b_in, q_gain, k_gain, w_proj_a, w_proj_b, w_out, ln_g, ln_b)
    biases = [_alibi_bias(group, dil) for group, (_, dil) in enumerate(A_GROUPS)]
    return (_trunk(x_prompt, params, biases), _trunk(x_sample, params, biases))
```

```python
import functools
import math

import jax
import jax.numpy as jnp
import numpy as np
from jax import lax
from jax.experimental import pallas as pl
from jax.experimental.pallas import tpu as pltpu

F32 = jnp.float32
BF16 = jnp.bfloat16

D_MODEL = 1024
DEPTH = 2
HEAD_DIM = 128
GRID_W = 64
A_GROUPS = ((128, 1), (512, 4), (2048, 16))
A_HEADS_PER_GROUP = 4
A_HEADS = 12
A_WIDTH = 512
ALIBI_MAX_BIAS = 8.0
B_HEADS = 8
B_KV_HEADS = 2
B_GROUP = B_HEADS // B_KV_HEADS
ROPE_THETA = 10000.0
IN_WIDTH = 9728
RMS_EPS = 1e-6
LN_EPS = 1e-5
MASK_VALUE = -1e30
ALPHA = (2 * DEPTH) ** 0.25
QK_SCALE = 1.0 / math.sqrt(HEAD_DIM)
BQ_SCALE = QK_SCALE * math.log2(math.e)
ONES_ROWS = 16
LOGIT_BOUND = 60.0

COL_TILE = 512
N_COL_TILES = IN_WIDTH // COL_TILE
T_AQ, T_AK, T_AV, T_AG = 0, 3, 6, 9
T_BQ, T_BKV, T_BG = 10, 12, 13
T_GATE_A, T_GATE_B = 15, 17
BK_COL128 = T_BKV * COL_TILE // HEAD_DIM
BV_COL128 = BK_COL128 + B_KV_HEADS

N_SIDE = 64

VMEM_LIMIT_BYTES = 56 * 1024 * 1024

SUPER_TILE = 2048
IN_PROJ_CHUNK = 512
GATHER_STRIDE = 4
FLASH_Q_ROWS = 1024
FLASH_K_ROWS = 512
DIL_SUB = 128
DIL_UNROLL = 4
POST_ROWS = 512
POST_CHUNK = 256


def _sigmoid(x):
    return 0.5 * jnp.tanh(0.5 * x) + 0.5


def _in_proj_kernel(x_ref, w_ref, b_ref, qg_ref, kg_ref, cos_ref, sin_ref, avg_ref, o_ref, xb_ref, hs_ref,
                    hs2_ref):
    j = pl.program_id(1)
    tm = x_ref.shape[0]

    @pl.when(j == 0)
    def _():
        xb_ref[...] = x_ref[...].astype(BF16)

    lane_tiles = COL_TILE // HEAD_DIM
    n_chunks = tm // IN_PROJ_CHUNK

    def column_tile(epilogue):
        def run():
            for rc in range(n_chunks):
                rows = slice(rc * IN_PROJ_CHUNK, (rc + 1) * IN_PROJ_CHUNK)
                h = jnp.dot(xb_ref[rows, :], w_ref[...], preferred_element_type=F32) + b_ref[...]
                epilogue(rc, rows, h)
        return run

    def head_mean_squares(h):
        width = h.shape[1]
        return jnp.dot((h * h).astype(BF16), avg_ref[0:width, 0:width], preferred_element_type=F32)

    def rms_rope(hh, ms, gain, rows):
        y = hh * lax.rsqrt(ms + RMS_EPS) * gain
        return y * cos_ref[rows, :] + pltpu.roll(y, HEAD_DIM // 2, 1) * sin_ref[rows, :]

    a_scale = jnp.where(j < T_AK, QK_SCALE, 1.0)
    a_group = j % len(A_GROUPS)

    def plain(rc, rows, h):
        o_ref[rows, :] = (h * a_scale).astype(BF16)

    def class_major(dil):
        n = tm // dil
        per_chunk = IN_PROJ_CHUNK // dil

        def epilogue(rc, rows, h):
            for ln in range(lane_tiles):
                hs_ref[rc, ln] = h[:, ln * HEAD_DIM:(ln + 1) * HEAD_DIM] * a_scale
            src_ref, stride = hs_ref, dil
            if dil == GATHER_STRIDE * GATHER_STRIDE:
                quarter = IN_PROJ_CHUNK // GATHER_STRIDE
                for ln in range(lane_tiles):
                    for c4 in range(GATHER_STRIDE):
                        hs2_ref[rc, ln, c4 * quarter:(c4 + 1) * quarter, :] = (
                            hs_ref[rc, ln, pl.ds(c4, quarter, stride=GATHER_STRIDE), :])
                src_ref, stride = hs2_ref, GATHER_STRIDE
            for c in range(dil):
                if src_ref is hs_ref:
                    first = c
                else:
                    first = (c % GATHER_STRIDE) * (IN_PROJ_CHUNK // GATHER_STRIDE) + c // GATHER_STRIDE
                dst = slice(c * n + rc * per_chunk, c * n + (rc + 1) * per_chunk)
                for ln in range(lane_tiles):
                    o_ref[dst, ln * HEAD_DIM:(ln + 1) * HEAD_DIM] = (
                        src_ref[rc, ln, pl.ds(first, per_chunk, stride=stride), :].astype(BF16))
        return epilogue

    def silu(rc, rows, h):
        o_ref[rows, :] = (h * _sigmoid(h)).astype(BF16)

    def rope_q(rc, rows, h):
        ms = head_mean_squares(h)
        for hd in range(lane_tiles):
            sl = slice(hd * HEAD_DIM, (hd + 1) * HEAD_DIM)
            o_ref[rows, sl] = (rms_rope(h[:, sl], ms[:, sl], qg_ref[...], rows) * BQ_SCALE).astype(BF16)

    def rope_k_and_v(rc, rows, h):
        k_width = B_KV_HEADS * HEAD_DIM
        ms = head_mean_squares(h[:, :k_width])
        for hd in range(B_KV_HEADS):
            sl = slice(hd * HEAD_DIM, (hd + 1) * HEAD_DIM)
            o_ref[rows, sl] = rms_rope(h[:, sl], ms[:, sl], kg_ref[...], rows).astype(BF16)
        o_ref[rows, k_width:] = h[:, k_width:].astype(BF16)

    def gate(rc, rows, h):
        o_ref[rows, :] = _sigmoid(h).astype(BF16)

    for group, (_, dil) in enumerate(A_GROUPS):
        pl.when((j < T_AG) & (a_group == group))(column_tile(plain if dil == 1 else class_major(dil)))
    pl.when((j == T_AG) | (j == T_BG) | (j == T_BG + 1))(column_tile(silu))
    pl.when((j == T_BQ) | (j == T_BQ + 1))(column_tile(rope_q))
    pl.when(j == T_BKV)(column_tile(rope_k_and_v))
    pl.when(j >= T_GATE_A)(column_tile(gate))


def _in_proj(x, w, b, q_gain, k_gain, cos, sin, seq, layer):
    tokens = x.shape[0]
    tm = SUPER_TILE
    assert seq % tm == 0
    pos_blocks = seq // tm
    row_spec = lambda width: pl.BlockSpec((tm, width), lambda i, j: (i, 0))
    pos_spec = pl.BlockSpec((tm, HEAD_DIM), lambda i, j: (i % pos_blocks, 0))
    vec_spec = pl.BlockSpec((None, 1, HEAD_DIM), lambda i, j: (layer, 0, 0))
    head_of_col = np.arange(COL_TILE) // HEAD_DIM
    head_average = jnp.asarray((head_of_col[:, None] == head_of_col[None, :]) / HEAD_DIM, dtype=BF16)
    return pl.pallas_call(
        _in_proj_kernel,
        grid=(tokens // tm, N_COL_TILES),
        in_specs=[
            row_spec(D_MODEL),
            pl.BlockSpec((None, None, D_MODEL, COL_TILE), lambda i, j: (layer, j, 0, 0)),
            pl.BlockSpec((None, 1, COL_TILE), lambda i, j: (layer, 0, j)),
            vec_spec, vec_spec, pos_spec, pos_spec,
            pl.BlockSpec((COL_TILE, COL_TILE), lambda i, j: (0, 0)),
        ],
        out_specs=pl.BlockSpec((tm, COL_TILE), lambda i, j: (i, j)),
        out_shape=jax.ShapeDtypeStruct((tokens, IN_WIDTH), BF16),
        scratch_shapes=[
            pltpu.VMEM((tm, D_MODEL), BF16),
            pltpu.VMEM((tm // IN_PROJ_CHUNK, COL_TILE // HEAD_DIM, IN_PROJ_CHUNK, HEAD_DIM), F32),
            pltpu.VMEM((tm // IN_PROJ_CHUNK, COL_TILE // HEAD_DIM, IN_PROJ_CHUNK, HEAD_DIM), F32),
        ],
        compiler_params=pltpu.CompilerParams(
            dimension_semantics=("arbitrary", "arbitrary"), vmem_limit_bytes=VMEM_LIMIT_BYTES),
        name="in_proj",
    )(x, w, b, q_gain, k_gain, cos, sin, head_average)


def _flash_kernel(q_ref, k_ref, v_ref, g_ref, o_ref, qt_ref, vt_ref, m_ref, acc_ref, k2max_ref,
                  *, tq, tk, n_chunks):
    i = pl.program_id(2)

    @pl.when(i == 0)
    def _():
        def fill(c, k2max):
            start = pl.multiple_of(c * tk, tk)
            vt_ref[c, 0:HEAD_DIM, :] = v_ref[pl.ds(start, tk), :].astype(F32).T.astype(BF16)
            vt_ref[c, HEAD_DIM:, :] = jnp.ones((ONES_ROWS, tk), BF16)
            kf = k_ref[pl.ds(start, tk), :].astype(F32)
            k2 = jnp.max(jnp.sum(kf * kf, axis=-1, keepdims=True), axis=0, keepdims=True)
            return jnp.maximum(k2max, k2)

        k2max_ref[...] = lax.fori_loop(0, n_chunks, fill, jnp.zeros((1, 1), F32))

    q2 = jnp.zeros((1, 1), F32)
    for hd in range(B_GROUP):
        qf = q_ref[:, hd * HEAD_DIM:(hd + 1) * HEAD_DIM].astype(F32).T
        qt_ref[:, hd * tq:(hd + 1) * tq] = qf.astype(BF16)
        q2 = jnp.maximum(q2, jnp.max(jnp.sum(qf * qf, axis=0, keepdims=True), axis=1, keepdims=True))
    acc_ref[...] = jnp.zeros(acc_ref.shape, F32)

    logits_bounded = (q2 * k2max_ref[...])[0, 0] <= LOGIT_BOUND * LOGIT_BOUND

    def scores(c):
        start = pl.multiple_of(c * tk, tk)
        return jnp.dot(k_ref[pl.ds(start, tk), :], qt_ref[...], preferred_element_type=F32)

    def bounded_body(c, carry):
        pt = jnp.exp2(scores(c)).astype(BF16)
        acc_ref[...] += jnp.dot(vt_ref[c], pt, preferred_element_type=F32)
        return carry

    def online_body(c, carry):
        st = scores(c)
        m_prev = m_ref[...]
        m_new = jnp.maximum(m_prev, jnp.max(st, axis=0, keepdims=True))
        alpha = jnp.exp2(m_prev - m_new)
        pt = jnp.exp2(st - m_new).astype(BF16)
        acc_ref[...] = alpha * acc_ref[...] + jnp.dot(vt_ref[c], pt, preferred_element_type=F32)
        m_ref[...] = m_new
        return carry

    @pl.when(logits_bounded)
    def _():
        lax.fori_loop(0, n_chunks, bounded_body, 0, unroll=2)

    @pl.when(jnp.logical_not(logits_bounded))
    def _():
        m_ref[...] = jnp.full(m_ref.shape, MASK_VALUE, F32)
        lax.fori_loop(0, n_chunks, online_body, 0, unroll=2)

    acc = acc_ref[...]
    ot = acc[0:HEAD_DIM, :] / acc[HEAD_DIM:HEAD_DIM + 1, :]
    for hd in range(B_GROUP):
        sl = slice(hd * HEAD_DIM, (hd + 1) * HEAD_DIM)
        o_ref[:, sl] = (ot[:, hd * tq:(hd + 1) * tq].T * g_ref[:, sl].astype(F32)).astype(BF16)


def _gqa_flash(h3):
    batch, seq, _ = h3.shape
    tq, tk = FLASH_Q_ROWS, FLASH_K_ROWS
    n_chunks = seq // tk
    kernel = functools.partial(_flash_kernel, tq=tq, tk=tk, n_chunks=n_chunks)
    return pl.pallas_call(
        kernel,
        grid=(batch, B_KV_HEADS, seq // tq),
        in_specs=[
            pl.BlockSpec((None, tq, COL_TILE), lambda b, g, i: (b, i, T_BQ + g)),
            pl.BlockSpec((None, seq, HEAD_DIM), lambda b, g, i: (b, 0, BK_COL128 + g)),
            pl.BlockSpec((None, seq, HEAD_DIM), lambda b, g, i: (b, 0, BV_COL128 + g)),
            pl.BlockSpec((None, tq, COL_TILE), lambda b, g, i: (b, i, T_BG + g)),
        ],
        out_specs=pl.BlockSpec((None, tq, COL_TILE), lambda b, g, i: (b, i, g)),
        out_shape=jax.ShapeDtypeStruct((batch, seq, B_HEADS * HEAD_DIM), BF16),
        scratch_shapes=[
            pltpu.VMEM((HEAD_DIM, B_GROUP * tq), BF16),
            pltpu.VMEM((n_chunks, HEAD_DIM + ONES_ROWS, tk), BF16),
            pltpu.VMEM((1, B_GROUP * tq), F32),
            pltpu.VMEM((HEAD_DIM + ONES_ROWS, B_GROUP * tq), F32),
            pltpu.VMEM((1, 1), F32),
        ],
        compiler_params=pltpu.CompilerParams(
            dimension_semantics=("arbitrary", "arbitrary", "arbitrary"), vmem_limit_bytes=VMEM_LIMIT_BYTES),
        name="gqa_flash",
    )(h3, h3, h3, h3)


def _strided_rows(start, size, stride):
    return pl.ds(start, size) if stride == 1 else pl.ds(start, size, stride=stride)


def _dilated_kernel(q_ref, k_ref, v_ref, kp_ref, kn_ref, vp_ref, vn_ref, bias_ref, o_ref, lse_ref,
                    kw_ref, vw_ref, stage_ref, *, dil, length):
    n = SUPER_TILE // dil
    win = DIL_SUB + 2 * N_SIDE
    first_pos = pl.program_id(1) * n
    lanes_per_head = HEAD_DIM // A_HEADS_PER_GROUP

    def sub_block(c, a):
        r0 = pl.multiple_of(a * DIL_SUB, DIL_SUB)
        key_pos = first_pos + a * DIL_SUB - N_SIDE + lax.broadcasted_iota(jnp.int32, (1, win), 1)
        key_ok = (key_pos >= 0) & (key_pos < length)
        token_rows = _strided_rows(c + dil * r0, DIL_SUB, dil)
        lse_cols = []
        for hd in range(A_HEADS_PER_GROUP):
            sl = slice(hd * HEAD_DIM, (hd + 1) * HEAD_DIM)
            q = q_ref[pl.ds(pl.multiple_of(c * n + r0, DIL_SUB), DIL_SUB), sl]
            k = kw_ref[c, pl.ds(r0, win), sl]
            v = vw_ref[c, pl.ds(r0, win), sl]
            s = lax.dot_general(q, k, (((1,), (1,)), ((), ())), preferred_element_type=F32) + bias_ref[hd]
            s = jnp.where(key_ok, s, MASK_VALUE)
            m = jnp.max(s, axis=-1, keepdims=True)
            p = jnp.exp(s - m)
            l = jnp.sum(p, axis=-1, keepdims=True)
            stage_ref[hd, token_rows, :] = jnp.dot(p.astype(BF16), v, preferred_element_type=F32) / l
            lse_cols.append(jnp.broadcast_to(m + jnp.log(l), (DIL_SUB, lanes_per_head)))
        lse_ref[token_rows, :] = jnp.concatenate(lse_cols, axis=-1)

    for c in range(dil):
        rows = slice(c * n, (c + 1) * n)
        kw_ref[c, 0:N_SIDE, :] = kp_ref[c]
        kw_ref[c, N_SIDE:N_SIDE + n, :] = k_ref[rows, :]
        kw_ref[c, N_SIDE + n:, :] = kn_ref[c]
        vw_ref[c, 0:N_SIDE, :] = vp_ref[c]
        vw_ref[c, N_SIDE:N_SIDE + n, :] = v_ref[rows, :]
        vw_ref[c, N_SIDE + n:, :] = vn_ref[c]

    subs_per_class = n // DIL_SUB

    def one_sub_block(sb, carry):
        sub_block(sb // subs_per_class, sb % subs_per_class)
        return carry

    lax.fori_loop(0, SUPER_TILE // DIL_SUB, one_sub_block, 0, unroll=DIL_UNROLL)

    for hd in range(A_HEADS_PER_GROUP):
        o_ref[:, hd * HEAD_DIM:(hd + 1) * HEAD_DIM] = stage_ref[hd].astype(BF16)


def _dilated_group(h3, bias, group, dil):
    batch, seq, _ = h3.shape
    tiles = seq // SUPER_TILE
    n = SUPER_TILE // dil
    halos = n // N_SIDE
    h6 = h3.reshape(batch, tiles, dil, halos, N_SIDE, IN_WIDTH)

    def cur(tile):
        return pl.BlockSpec((None, SUPER_TILE, COL_TILE), lambda b, t: (b, t, tile + group))

    def prev(tile):
        return pl.BlockSpec((None, None, dil, None, N_SIDE, COL_TILE),
                            lambda b, t: (b, jnp.maximum(t - 1, 0), 0, halos - 1, 0, tile + group))

    def nxt(tile):
        return pl.BlockSpec((None, None, dil, None, N_SIDE, COL_TILE),
                            lambda b, t: (b, jnp.minimum(t + 1, tiles - 1), 0, 0, 0, tile + group))

    kernel = functools.partial(_dilated_kernel, dil=dil, length=seq // dil)
    o, lse = pl.pallas_call(
        kernel,
        grid=(batch, tiles),
        in_specs=[
            cur(T_AQ), cur(T_AK), cur(T_AV), prev(T_AK), nxt(T_AK), prev(T_AV), nxt(T_AV),
            pl.BlockSpec((A_HEADS_PER_GROUP, DIL_SUB, DIL_SUB + 2 * N_SIDE), lambda b, t: (0, 0, 0)),
        ],
        out_specs=[
            pl.BlockSpec((None, SUPER_TILE, A_WIDTH), lambda b, t: (b, t, 0)),
            pl.BlockSpec((None, SUPER_TILE, HEAD_DIM), lambda b, t: (b, t, 0)),
        ],
        out_shape=[
            jax.ShapeDtypeStruct((batch, seq, A_WIDTH), BF16),
            jax.ShapeDtypeStruct((batch, seq, HEAD_DIM), F32),
        ],
        scratch_shapes=[
            pltpu.VMEM((dil, n + 2 * N_SIDE, COL_TILE), BF16),
            pltpu.VMEM((dil, n + 2 * N_SIDE, COL_TILE), BF16),
            pltpu.VMEM((A_HEADS_PER_GROUP, SUPER_TILE, HEAD_DIM), F32),
        ],
        compiler_params=pltpu.CompilerParams(
            dimension_semantics=("arbitrary", "arbitrary"), vmem_limit_bytes=VMEM_LIMIT_BYTES),
        name=f"dilated_g{group}",
    )(h3, h3, h3, h6, h6, h6, h6, bias)
    return o.reshape(batch * seq, A_WIDTH), lse.reshape(batch * seq, HEAD_DIM)


def _post_kernel(x_ref, o0_ref, o1_ref, o2_ref, l0_ref, l1_ref, l2_ref, ag_ref, yb_ref,
                 ga0_ref, ga1_ref, gb0_ref, gb1_ref, wpa_ref, wpb_ref, wo_ref, lng_ref, lnb_ref, y_ref):
    lanes_per_head = HEAD_DIM // A_HEADS_PER_GROUP
    for rc in range(x_ref.shape[0] // POST_CHUNK):
        rows = slice(rc * POST_CHUNK, (rc + 1) * POST_CHUNK)
        l0, l1, l2 = l0_ref[rows, :], l1_ref[rows, :], l2_ref[rows, :]
        top = jnp.maximum(jnp.maximum(l0, l1), l2)
        e0, e1, e2 = jnp.exp(l0 - top), jnp.exp(l1 - top), jnp.exp(l2 - top)
        inv = 1.0 / (e0 + e1 + e2)
        ya_heads = []
        for hd in range(A_HEADS_PER_GROUP):
            sl = slice(hd * HEAD_DIM, (hd + 1) * HEAD_DIM)
            pick = slice(hd * lanes_per_head, hd * lanes_per_head + 1)
            w0 = jnp.broadcast_to((e0 * inv)[:, pick], (POST_CHUNK, HEAD_DIM))
            w1 = jnp.broadcast_to((e1 * inv)[:, pick], (POST_CHUNK, HEAD_DIM))
            w2 = jnp.broadcast_to((e2 * inv)[:, pick], (POST_CHUNK, HEAD_DIM))
            mix = (w0 * o0_ref[rows, sl].astype(F32) + w1 * o1_ref[rows, sl].astype(F32)
                   + w2 * o2_ref[rows, sl].astype(F32))
            ya_heads.append((mix * ag_ref[rows, sl].astype(F32)).astype(BF16))
        ya = jnp.concatenate(ya_heads, axis=-1)
        pa = jnp.dot(ya, wpa_ref[...], preferred_element_type=F32)
        pb = jnp.dot(yb_ref[rows, :], wpb_ref[...], preferred_element_type=F32)
        gate_a = jnp.concatenate([ga0_ref[rows, :], ga1_ref[rows, :]], axis=-1).astype(F32)
        gate_b = jnp.concatenate([gb0_ref[rows, :], gb1_ref[rows, :]], axis=-1).astype(F32)
        merged = (gate_a * pa + gate_b * pb).astype(BF16)
        z = ALPHA * x_ref[rows, :] + jnp.dot(merged, wo_ref[...], preferred_element_type=F32)
        mu = jnp.mean(z, axis=-1, keepdims=True)
        zc = z - mu
        var = jnp.mean(zc * zc, axis=-1, keepdims=True)
        y_ref[rows, :] = zc * lax.rsqrt(var + LN_EPS) * lng_ref[...] + lnb_ref[...]


def _post(x, outs, lses, h, yb, wpa, wpb, wo, ln_g, ln_b, layer):
    tokens = x.shape[0]
    tm = POST_ROWS
    row = lambda width, col=0: pl.BlockSpec((tm, width), lambda i, col=col: (i, col))
    full = lambda a: pl.BlockSpec((None,) + a.shape[1:], lambda i: (layer, 0, 0))
    return pl.pallas_call(
        _post_kernel,
        grid=(tokens // tm,),
        in_specs=[
            row(D_MODEL),
            row(A_WIDTH), row(A_WIDTH), row(A_WIDTH),
            row(HEAD_DIM), row(HEAD_DIM), row(HEAD_DIM),
            row(COL_TILE, T_AG),
            row(B_HEADS * HEAD_DIM),
            row(COL_TILE, T_GATE_A), row(COL_TILE, T_GATE_A + 1),
            row(COL_TILE, T_GATE_B), row(COL_TILE, T_GATE_B + 1),
            full(wpa), full(wpb), full(wo), full(ln_g), full(ln_b),
        ],
        out_specs=row(D_MODEL),
        out_shape=jax.ShapeDtypeStruct((tokens, D_MODEL), F32),
        compiler_params=pltpu.CompilerParams(
            dimension_semantics=("arbitrary",), vmem_limit_bytes=VMEM_LIMIT_BYTES),
        name="post",
    )(x, *outs, *lses, h, yb, h, h, h, h, wpa, wpb, wo, ln_g, ln_b)


def _rope_tables(seq):
    rows = seq // GRID_W
    row_pos = jnp.repeat(jnp.arange(rows), GRID_W).astype(F32)
    col_pos = jnp.tile(jnp.arange(GRID_W), rows).astype(F32)
    axis_dim = HEAD_DIM // 2
    inv_freq = ROPE_THETA ** (-jnp.arange(0, axis_dim, 2, dtype=F32) / axis_dim)
    ang_r = row_pos[:, None] * inv_freq[None]
    ang_c = col_pos[:, None] * inv_freq[None]
    cos = jnp.cos(jnp.concatenate([ang_r, ang_c, ang_r, ang_c], axis=-1))
    sin_r, sin_c = jnp.sin(ang_r), jnp.sin(ang_c)
    return cos, jnp.concatenate([-sin_r, -sin_c, sin_r, sin_c], axis=-1)


def _pair_rotary_halves(a, first_col, n_heads):
    seg = a[..., first_col:first_col + n_heads * HEAD_DIM]
    seg = seg.reshape(*seg.shape[:-1], n_heads, 2, 2, HEAD_DIM // 4)
    seg = jnp.swapaxes(seg, -3, -2).reshape(*a.shape[:-1], n_heads * HEAD_DIM)
    return jnp.concatenate([a[..., :first_col], seg, a[..., first_col + n_heads * HEAD_DIM:]], axis=-1)


def _alibi_bias(group, dil):
    slopes = 2.0 ** (-ALIBI_MAX_BIAS * np.arange(1, A_HEADS + 1) / A_HEADS)
    slopes = slopes[group * A_HEADS_PER_GROUP:(group + 1) * A_HEADS_PER_GROUP]
    rel = np.arange(DIL_SUB + 2 * N_SIDE)[None, :] - N_SIDE - np.arange(DIL_SUB)[:, None]
    inside = np.abs(rel) <= N_SIDE
    bias = -slopes[:, None, None] * (dil * np.abs(rel)).astype(np.float32)[None]
    return jnp.asarray(np.where(inside[None], bias, MASK_VALUE), dtype=F32)


def _trunk(x, params, biases, cos, sin):
    batch, seq, _ = x.shape
    xf = x.reshape(batch * seq, D_MODEL)
    w_in, b_in, q_gain, k_gain, wpa, wpb, wo, ln_g, ln_b = params
    for layer in range(DEPTH):
        h = _in_proj(xf, w_in, b_in, q_gain, k_gain, cos, sin, seq, layer)
        h3 = h.reshape(batch, seq, IN_WIDTH)
        yb = _gqa_flash(h3).reshape(batch * seq, B_HEADS * HEAD_DIM)
        outs, lses = [], []
        for group, (_, dil) in enumerate(A_GROUPS):
            o, lse = _dilated_group(h3, biases[group], group, dil)
            outs.append(o)
            lses.append(lse)
        xf = _post(xf, outs, lses, h, yb, wpa, wpb, wo, ln_g, ln_b, layer)
    return xf.reshape(batch, seq, D_MODEL)


def _prepare_params(w_in, b_in, q_gain, k_gain, w_proj_a, w_proj_b, w_out, ln_g, ln_b):
    rotary_heads = B_HEADS + B_KV_HEADS
    first_rotary_col = T_BQ * COL_TILE
    w_in = _pair_rotary_halves(w_in, first_rotary_col, rotary_heads).astype(BF16)
    return (
        w_in.reshape(DEPTH, D_MODEL, N_COL_TILES, COL_TILE).transpose(0, 2, 1, 3),
        _pair_rotary_halves(b_in, first_rotary_col, rotary_heads).reshape(DEPTH, 1, IN_WIDTH),
        _pair_rotary_halves(q_gain, 0, 1).reshape(DEPTH, 1, HEAD_DIM),
        _pair_rotary_halves(k_gain, 0, 1).reshape(DEPTH, 1, HEAD_DIM),
        w_proj_a.astype(BF16),
        w_proj_b.astype(BF16),
        w_out.astype(BF16),
        ln_g.reshape(DEPTH, 1, D_MODEL),
        ln_b.reshape(DEPTH, 1, D_MODEL),
    )


def kernel(x_prompt, x_sample, w_in, b_in, q_gain, k_gain, w_proj_a, w_proj_b, w_out, ln_g, ln_b):
    params = _prepare_params(w_in, b_in, q_gain, k_gain, w_proj_a, w_proj_b, w_out, ln_g, ln_b)
    biases = [_alibi_bias(group, dil) for group, (_, dil) in enumerate(A_GROUPS)]
    cos, sin = _rope_tables(max(x_prompt.shape[1], x_sample.shape[1]))
    return (_trunk(x_prompt, params, biases, cos, sin), _trunk(x_sample, params, biases, cos, sin))
```

```python
import functools
import math

import jax
import jax.numpy as jnp
import numpy as np
from jax import lax
from jax.experimental import pallas as pl
from jax.experimental.pallas import tpu as pltpu

F32 = jnp.float32
BF16 = jnp.bfloat16

D_MODEL = 1024
DEPTH = 2
HEAD_DIM = 128
GRID_W = 64
A_GROUPS = ((128, 1), (512, 4), (2048, 16))
A_HEADS_PER_GROUP = 4
A_HEADS = 12
A_WIDTH = 512
ALIBI_MAX_BIAS = 8.0
B_HEADS = 8
B_KV_HEADS = 2
B_GROUP = B_HEADS // B_KV_HEADS
ROPE_THETA = 10000.0
IN_WIDTH = 9728
RMS_EPS = 1e-6
LN_EPS = 1e-5
MASK_VALUE = -1e30
ALPHA = (2 * DEPTH) ** 0.25
QK_SCALE = 1.0 / math.sqrt(HEAD_DIM)
BQ_SCALE = QK_SCALE * math.log2(math.e)
SUBLANES = 8
LOGIT_BOUND = 60.0

COL_TILE = 512
N_COL_TILES = IN_WIDTH // COL_TILE
T_AQ, T_AK, T_AV, T_AG = 0, 3, 6, 9
T_BQ, T_BKV, T_BG = 10, 12, 13
T_GATE_A, T_GATE_B = 15, 17
ROTARY_TILES = T_BKV + 1 - T_BQ
BK_COL128 = T_BKV * COL_TILE // HEAD_DIM
BV_COL128 = BK_COL128 + B_KV_HEADS

N_SIDE = 64

VMEM_LIMIT_BYTES = 56 * 1024 * 1024

SUPER_TILE = 2048
IN_PROJ_CHUNK = 512
GATHER_STRIDE = 4
FLASH_Q_ROWS = 1024
FLASH_K_ROWS = 1024
DIL_SUB = 128
DIL_UNROLL = 4
POST_ROWS = 512
POST_CHUNK = 256


def _sigmoid(x):
    return 0.5 * jnp.tanh(0.5 * x) + 0.5


def _in_proj_kernel(x_ref, w_ref, b_ref, wrot_ref, brot_ref, qg_ref, kg_ref, cos_ref, sin_ref, avg_ref,
                    o_ref, xb_ref, hs_ref, hs2_ref):
    j = pl.program_id(1)
    tm = x_ref.shape[0]

    @pl.when(j == 0)
    def _():
        xb_ref[...] = x_ref[...].astype(BF16)

    lane_tiles = COL_TILE // HEAD_DIM
    n_chunks = tm // IN_PROJ_CHUNK

    def column_tile(epilogue, weights=w_ref, bias=b_ref):
        def run():
            for rc in range(n_chunks):
                rows = slice(rc * IN_PROJ_CHUNK, (rc + 1) * IN_PROJ_CHUNK)
                h = jnp.dot(xb_ref[rows, :], weights[...], preferred_element_type=F32) + bias[...]
                epilogue(rc, rows, h)
        return run

    def head_mean_squares(h):
        width = h.shape[1]
        return jnp.dot((h * h).astype(BF16), avg_ref[0:width, 0:width], preferred_element_type=F32)

    def rms_rope(hh, ms, gain, rows):
        y = hh * lax.rsqrt(ms + RMS_EPS) * gain
        return y * cos_ref[rows, :] + pltpu.roll(y, HEAD_DIM // 2, 1) * sin_ref[rows, :]

    a_scale = jnp.where(j < T_AK, QK_SCALE, 1.0)
    a_group = j % len(A_GROUPS)

    def plain(rc, rows, h):
        o_ref[rows, :] = (h * a_scale).astype(BF16)

    def class_major(dil):
        n = tm // dil
        per_chunk = IN_PROJ_CHUNK // dil

        def epilogue(rc, rows, h):
            for ln in range(lane_tiles):
                hs_ref[rc, ln] = h[:, ln * HEAD_DIM:(ln + 1) * HEAD_DIM] * a_scale
            src_ref, stride = hs_ref, dil
            if dil == GATHER_STRIDE * GATHER_STRIDE:
                quarter = IN_PROJ_CHUNK // GATHER_STRIDE
                for ln in range(lane_tiles):
                    for c4 in range(GATHER_STRIDE):
                        hs2_ref[rc, ln, c4 * quarter:(c4 + 1) * quarter, :] = (
                            hs_ref[rc, ln, pl.ds(c4, quarter, stride=GATHER_STRIDE), :])
                src_ref, stride = hs2_ref, GATHER_STRIDE
            for c in range(dil):
                if src_ref is hs_ref:
                    first = c
                else:
                    first = (c % GATHER_STRIDE) * (IN_PROJ_CHUNK // GATHER_STRIDE) + c // GATHER_STRIDE
                dst = slice(c * n + rc * per_chunk, c * n + (rc + 1) * per_chunk)
                for ln in range(lane_tiles):
                    o_ref[dst, ln * HEAD_DIM:(ln + 1) * HEAD_DIM] = (
                        src_ref[rc, ln, pl.ds(first, per_chunk, stride=stride), :].astype(BF16))
        return epilogue

    def silu(rc, rows, h):
        o_ref[rows, :] = (h * _sigmoid(h)).astype(BF16)

    def rope_q(rc, rows, h):
        ms = head_mean_squares(h)
        for hd in range(lane_tiles):
            sl = slice(hd * HEAD_DIM, (hd + 1) * HEAD_DIM)
            o_ref[rows, sl] = (rms_rope(h[:, sl], ms[:, sl], qg_ref[...], rows) * BQ_SCALE).astype(BF16)

    def rope_k_and_v(rc, rows, h):
        k_width = B_KV_HEADS * HEAD_DIM
        ms = head_mean_squares(h[:, :k_width])
        for hd in range(B_KV_HEADS):
            sl = slice(hd * HEAD_DIM, (hd + 1) * HEAD_DIM)
            o_ref[rows, sl] = rms_rope(h[:, sl], ms[:, sl], kg_ref[...], rows).astype(BF16)
        o_ref[rows, k_width:] = h[:, k_width:].astype(BF16)

    def gate(rc, rows, h):
        o_ref[rows, :] = _sigmoid(h).astype(BF16)

    for group, (_, dil) in enumerate(A_GROUPS):
        pl.when((j < T_AG) & (a_group == group))(column_tile(plain if dil == 1 else class_major(dil)))
    pl.when((j == T_AG) | (j == T_BG) | (j == T_BG + 1))(column_tile(silu))
    pl.when((j == T_BQ) | (j == T_BQ + 1))(column_tile(rope_q, wrot_ref, brot_ref))
    pl.when(j == T_BKV)(column_tile(rope_k_and_v, wrot_ref, brot_ref))
    pl.when(j >= T_GATE_A)(column_tile(gate))


def _in_proj(x, w, b, w_rot, b_rot, q_gain, k_gain, cos, sin, seq, layer):
    tokens = x.shape[0]
    tm = SUPER_TILE
    assert seq % tm == 0
    pos_blocks = seq // tm
    row_spec = lambda width: pl.BlockSpec((tm, width), lambda i, j: (i, 0))
    pos_spec = pl.BlockSpec((tm, HEAD_DIM), lambda i, j: (i % pos_blocks, 0))
    vec_spec = pl.BlockSpec((None, 1, HEAD_DIM), lambda i, j: (layer, 0, 0))
    head_of_col = np.arange(COL_TILE) // HEAD_DIM
    head_average = jnp.asarray((head_of_col[:, None] == head_of_col[None, :]) / HEAD_DIM, dtype=BF16)
    return pl.pallas_call(
        _in_proj_kernel,
        grid=(tokens // tm, N_COL_TILES),
        in_specs=[
            row_spec(D_MODEL),
            pl.BlockSpec((None, None, D_MODEL, COL_TILE), lambda i, j: (layer, j, 0, 0)),
            pl.BlockSpec((None, 1, COL_TILE), lambda i, j: (layer, 0, j)),
            pl.BlockSpec((None, None, D_MODEL, COL_TILE),
                         lambda i, j: (layer, jnp.clip(j - T_BQ, 0, ROTARY_TILES - 1), 0, 0)),
            pl.BlockSpec((None, 1, COL_TILE), lambda i, j: (layer, 0, jnp.clip(j - T_BQ, 0, ROTARY_TILES - 1))),
            vec_spec, vec_spec, pos_spec, pos_spec,
            pl.BlockSpec((COL_TILE, COL_TILE), lambda i, j: (0, 0)),
        ],
        out_specs=pl.BlockSpec((tm, COL_TILE), lambda i, j: (i, j)),
        out_shape=jax.ShapeDtypeStruct((tokens, IN_WIDTH), BF16),
        scratch_shapes=[
            pltpu.VMEM((tm, D_MODEL), BF16),
            pltpu.VMEM((tm // IN_PROJ_CHUNK, COL_TILE // HEAD_DIM, IN_PROJ_CHUNK, HEAD_DIM), F32),
            pltpu.VMEM((tm // IN_PROJ_CHUNK, COL_TILE // HEAD_DIM, IN_PROJ_CHUNK, HEAD_DIM), F32),
        ],
        compiler_params=pltpu.CompilerParams(
            dimension_semantics=("arbitrary", "arbitrary"), vmem_limit_bytes=VMEM_LIMIT_BYTES),
        name="in_proj",
    )(x, w, b, w_rot, b_rot, q_gain, k_gain, cos, sin, head_average)


def _flash_kernel(q_ref, k_ref, v_ref, g_ref, o_ref, qt_ref, vt_ref, m_ref, l_ref, acc_ref, k2max_ref,
                  *, tq, tk, n_chunks):
    i = pl.program_id(2)

    @pl.when(i == 0)
    def _():
        def fill(c, k2max):
            start = pl.multiple_of(c * tk, tk)
            vt_ref[c] = v_ref[pl.ds(start, tk), :].astype(F32).T.astype(BF16)
            kf = k_ref[pl.ds(start, tk), :].astype(F32)
            k2 = jnp.max(jnp.sum(kf * kf, axis=-1, keepdims=True), axis=0, keepdims=True)
            return jnp.maximum(k2max, k2)

        k2max_ref[...] = lax.fori_loop(0, n_chunks, fill, jnp.zeros((1, 1), F32))

    q2 = jnp.zeros((1, 1), F32)
    for hd in range(B_GROUP):
        qf = q_ref[:, hd * HEAD_DIM:(hd + 1) * HEAD_DIM].astype(F32).T
        qt_ref[:, hd * tq:(hd + 1) * tq] = qf.astype(BF16)
        q2 = jnp.maximum(q2, jnp.max(jnp.sum(qf * qf, axis=0, keepdims=True), axis=1, keepdims=True))
    acc_ref[...] = jnp.zeros(acc_ref.shape, F32)
    l_ref[...] = jnp.zeros(l_ref.shape, F32)

    logits_bounded = (q2 * k2max_ref[...])[0, 0] <= LOGIT_BOUND * LOGIT_BOUND

    def scores(c):
        start = pl.multiple_of(c * tk, tk)
        return jnp.dot(k_ref[pl.ds(start, tk), :], qt_ref[...], preferred_element_type=F32)

    def sublane_partial_sums(p):
        return jnp.sum(p.reshape(tk // SUBLANES, SUBLANES, p.shape[1]), axis=0)

    def bounded_body(c, carry):
        p = jnp.exp2(scores(c))
        l_ref[...] += sublane_partial_sums(p)
        acc_ref[...] += jnp.dot(vt_ref[c], p.astype(BF16), preferred_element_type=F32)
        return carry

    def online_body(c, carry):
        st = scores(c)
        m_prev = m_ref[...]
        m_new = jnp.maximum(m_prev, jnp.max(st, axis=0, keepdims=True))
        alpha = jnp.exp2(m_prev - m_new)
        p = jnp.exp2(st - m_new)
        l_ref[...] = alpha * l_ref[...] + sublane_partial_sums(p)
        acc_ref[...] = alpha * acc_ref[...] + jnp.dot(vt_ref[c], p.astype(BF16), preferred_element_type=F32)
        m_ref[...] = m_new
        return carry

    @pl.when(logits_bounded)
    def _():
        lax.fori_loop(0, n_chunks, bounded_body, 0, unroll=2)

    @pl.when(jnp.logical_not(logits_bounded))
    def _():
        m_ref[...] = jnp.full(m_ref.shape, MASK_VALUE, F32)
        lax.fori_loop(0, n_chunks, online_body, 0, unroll=2)

    ot = acc_ref[...] / jnp.sum(l_ref[...], axis=0, keepdims=True)
    for hd in range(B_GROUP):
        sl = slice(hd * HEAD_DIM, (hd + 1) * HEAD_DIM)
        o_ref[:, sl] = (ot[:, hd * tq:(hd + 1) * tq].T * g_ref[:, sl].astype(F32)).astype(BF16)


def _gqa_flash(h3):
    batch, seq, _ = h3.shape
    tq, tk = FLASH_Q_ROWS, FLASH_K_ROWS
    n_chunks = seq // tk
    kernel = functools.partial(_flash_kernel, tq=tq, tk=tk, n_chunks=n_chunks)
    return pl.pallas_call(
        kernel,
        grid=(batch, B_KV_HEADS, seq // tq),
        in_specs=[
            pl.BlockSpec((None, tq, COL_TILE), lambda b, g, i: (b, i, T_BQ + g)),
            pl.BlockSpec((None, seq, HEAD_DIM), lambda b, g, i: (b, 0, BK_COL128 + g)),
            pl.BlockSpec((None, seq, HEAD_DIM), lambda b, g, i: (b, 0, BV_COL128 + g)),
            pl.BlockSpec((None, tq, COL_TILE), lambda b, g, i: (b, i, T_BG + g)),
        ],
        out_specs=pl.BlockSpec((None, tq, COL_TILE), lambda b, g, i: (b, i, g)),
        out_shape=jax.ShapeDtypeStruct((batch, seq, B_HEADS * HEAD_DIM), BF16),
        scratch_shapes=[
            pltpu.VMEM((HEAD_DIM, B_GROUP * tq), BF16),
            pltpu.VMEM((n_chunks, HEAD_DIM, tk), BF16),
            pltpu.VMEM((1, B_GROUP * tq), F32),
            pltpu.VMEM((SUBLANES, B_GROUP * tq), F32),
            pltpu.VMEM((HEAD_DIM, B_GROUP * tq), F32),
            pltpu.VMEM((1, 1), F32),
        ],
        compiler_params=pltpu.CompilerParams(
            dimension_semantics=("arbitrary", "arbitrary", "arbitrary"), vmem_limit_bytes=VMEM_LIMIT_BYTES),
        name="gqa_flash",
    )(h3, h3, h3, h3)


def _strided_rows(start, size, stride):
    return pl.ds(start, size) if stride == 1 else pl.ds(start, size, stride=stride)


def _dilated_kernel(q_ref, k_ref, v_ref, kp_ref, kn_ref, vp_ref, vn_ref, bias_ref, o_ref, lse_ref,
                    kw_ref, vw_ref, stage_ref, *, dil, length):
    n = SUPER_TILE // dil
    win = DIL_SUB + 2 * N_SIDE
    first_pos = pl.program_id(1) * n
    lanes_per_head = HEAD_DIM // A_HEADS_PER_GROUP

    def sub_block(c, a):
        r0 = pl.multiple_of(a * DIL_SUB, DIL_SUB)
        key_pos = first_pos + a * DIL_SUB - N_SIDE + lax.broadcasted_iota(jnp.int32, (1, win), 1)
        key_ok = (key_pos >= 0) & (key_pos < length)
        token_rows = _strided_rows(c + dil * r0, DIL_SUB, dil)
        lse_cols = []
        for hd in range(A_HEADS_PER_GROUP):
            sl = slice(hd * HEAD_DIM, (hd + 1) * HEAD_DIM)
            q = q_ref[pl.ds(pl.multiple_of(c * n + r0, DIL_SUB), DIL_SUB), sl]
            k = kw_ref[c, pl.ds(r0, win), sl]
            v = vw_ref[c, pl.ds(r0, win), sl]
            s = lax.dot_general(q, k, (((1,), (1,)), ((), ())), preferred_element_type=F32) + bias_ref[hd]
            s = jnp.where(key_ok, s, MASK_VALUE)
            m = jnp.max(s, axis=-1, keepdims=True)
            p = jnp.exp(s - m)
            l = jnp.sum(p, axis=-1, keepdims=True)
            stage_ref[hd, token_rows, :] = jnp.dot(p.astype(BF16), v, preferred_element_type=F32) / l
            lse_cols.append(jnp.broadcast_to(m + jnp.log(l), (DIL_SUB, lanes_per_head)))
        lse_ref[token_rows, :] = jnp.concatenate(lse_cols, axis=-1)

    for c in range(dil):
        rows = slice(c * n, (c + 1) * n)
        kw_ref[c, 0:N_SIDE, :] = kp_ref[c]
        kw_ref[c, N_SIDE:N_SIDE + n, :] = k_ref[rows, :]
        kw_ref[c, N_SIDE + n:, :] = kn_ref[c]
        vw_ref[c, 0:N_SIDE, :] = vp_ref[c]
        vw_ref[c, N_SIDE:N_SIDE + n, :] = v_ref[rows, :]
        vw_ref[c, N_SIDE + n:, :] = vn_ref[c]

    subs_per_class = n // DIL_SUB

    def one_sub_block(sb, carry):
        sub_block(sb // subs_per_class, sb % subs_per_class)
        return carry

    lax.fori_loop(0, SUPER_TILE // DIL_SUB, one_sub_block, 0, unroll=DIL_UNROLL)

    for hd in range(A_HEADS_PER_GROUP):
        o_ref[:, hd * HEAD_DIM:(hd + 1) * HEAD_DIM] = stage_ref[hd].astype(BF16)


def _dilated_group(h3, bias, group, dil):
    batch, seq, _ = h3.shape
    tiles = seq // SUPER_TILE
    n = SUPER_TILE // dil
    halos = n // N_SIDE
    h6 = h3.reshape(batch, tiles, dil, halos, N_SIDE, IN_WIDTH)

    def cur(tile):
        return pl.BlockSpec((None, SUPER_TILE, COL_TILE), lambda b, t: (b, t, tile + group))

    def prev(tile):
        return pl.BlockSpec((None, None, dil, None, N_SIDE, COL_TILE),
                            lambda b, t: (b, jnp.maximum(t - 1, 0), 0, halos - 1, 0, tile + group))

    def nxt(tile):
        return pl.BlockSpec((None, None, dil, None, N_SIDE, COL_TILE),
                            lambda b, t: (b, jnp.minimum(t + 1, tiles - 1), 0, 0, 0, tile + group))

    kernel = functools.partial(_dilated_kernel, dil=dil, length=seq // dil)
    o, lse = pl.pallas_call(
        kernel,
        grid=(batch, tiles),
        in_specs=[
            cur(T_AQ), cur(T_AK), cur(T_AV), prev(T_AK), nxt(T_AK), prev(T_AV), nxt(T_AV),
            pl.BlockSpec((A_HEADS_PER_GROUP, DIL_SUB, DIL_SUB + 2 * N_SIDE), lambda b, t: (0, 0, 0)),
        ],
        out_specs=[
            pl.BlockSpec((None, SUPER_TILE, A_WIDTH), lambda b, t: (b, t, 0)),
            pl.BlockSpec((None, SUPER_TILE, HEAD_DIM), lambda b, t: (b, t, 0)),
        ],
        out_shape=[
            jax.ShapeDtypeStruct((batch, seq, A_WIDTH), BF16),
            jax.ShapeDtypeStruct((batch, seq, HEAD_DIM), F32),
        ],
        scratch_shapes=[
            pltpu.VMEM((dil, n + 2 * N_SIDE, COL_TILE), BF16),
            pltpu.VMEM((dil, n + 2 * N_SIDE, COL_TILE), BF16),
            pltpu.VMEM((A_HEADS_PER_GROUP, SUPER_TILE, HEAD_DIM), F32),
        ],
        compiler_params=pltpu.CompilerParams(
            dimension_semantics=("arbitrary", "arbitrary"), vmem_limit_bytes=VMEM_LIMIT_BYTES),
        name=f"dilated_g{group}",
    )(h3, h3, h3, h6, h6, h6, h6, bias)
    return o.reshape(batch * seq, A_WIDTH), lse.reshape(batch * seq, HEAD_DIM)


def _post_kernel(x_ref, o0_ref, o1_ref, o2_ref, l0_ref, l1_ref, l2_ref, ag_ref, yb_ref,
                 ga0_ref, ga1_ref, gb0_ref, gb1_ref, wpa_ref, wpb_ref, wo_ref, lng_ref, lnb_ref, y_ref):
    lanes_per_head = HEAD_DIM // A_HEADS_PER_GROUP
    for rc in range(x_ref.shape[0] // POST_CHUNK):
        rows = slice(rc * POST_CHUNK, (rc + 1) * POST_CHUNK)
        l0, l1, l2 = l0_ref[rows, :], l1_ref[rows, :], l2_ref[rows, :]
        top = jnp.maximum(jnp.maximum(l0, l1), l2)
        e0, e1, e2 = jnp.exp(l0 - top), jnp.exp(l1 - top), jnp.exp(l2 - top)
        inv = 1.0 / (e0 + e1 + e2)
        ya_heads = []
        for hd in range(A_HEADS_PER_GROUP):
            sl = slice(hd * HEAD_DIM, (hd + 1) * HEAD_DIM)
            pick = slice(hd * lanes_per_head, hd * lanes_per_head + 1)
            w0 = jnp.broadcast_to((e0 * inv)[:, pick], (POST_CHUNK, HEAD_DIM))
            w1 = jnp.broadcast_to((e1 * inv)[:, pick], (POST_CHUNK, HEAD_DIM))
            w2 = jnp.broadcast_to((e2 * inv)[:, pick], (POST_CHUNK, HEAD_DIM))
            mix = (w0 * o0_ref[rows, sl].astype(F32) + w1 * o1_ref[rows, sl].astype(F32)
                   + w2 * o2_ref[rows, sl].astype(F32))
            ya_heads.append((mix * ag_ref[rows, sl].astype(F32)).astype(BF16))
        ya = jnp.concatenate(ya_heads, axis=-1)
        pa = jnp.dot(ya, wpa_ref[...], preferred_element_type=F32)
        pb = jnp.dot(yb_ref[rows, :], wpb_ref[...], preferred_element_type=F32)
        gate_a = jnp.concatenate([ga0_ref[rows, :], ga1_ref[rows, :]], axis=-1).astype(F32)
        gate_b = jnp.concatenate([gb0_ref[rows, :], gb1_ref[rows, :]], axis=-1).astype(F32)
        merged = (gate_a * pa + gate_b * pb).astype(BF16)
        z = ALPHA * x_ref[rows, :] + jnp.dot(merged, wo_ref[...], preferred_element_type=F32)
        mu = jnp.mean(z, axis=-1, keepdims=True)
        zc = z - mu
        var = jnp.mean(zc * zc, axis=-1, keepdims=True)
        y_ref[rows, :] = zc * lax.rsqrt(var + LN_EPS) * lng_ref[...] + lnb_ref[...]


def _post(x, outs, lses, h, yb, wpa, wpb, wo, ln_g, ln_b, layer):
    tokens = x.shape[0]
    tm = POST_ROWS
    row = lambda width, col=0: pl.BlockSpec((tm, width), lambda i, col=col: (i, col))
    full = lambda a: pl.BlockSpec((None,) + a.shape[1:], lambda i: (layer, 0, 0))
    return pl.pallas_call(
        _post_kernel,
        grid=(tokens // tm,),
        in_specs=[
            row(D_MODEL),
            row(A_WIDTH), row(A_WIDTH), row(A_WIDTH),
            row(HEAD_DIM), row(HEAD_DIM), row(HEAD_DIM),
            row(COL_TILE, T_AG),
            row(B_HEADS * HEAD_DIM),
            row(COL_TILE, T_GATE_A), row(COL_TILE, T_GATE_A + 1),
            row(COL_TILE, T_GATE_B), row(COL_TILE, T_GATE_B + 1),
            full(wpa), full(wpb), full(wo), full(ln_g), full(ln_b),
        ],
        out_specs=row(D_MODEL),
        out_shape=jax.ShapeDtypeStruct((tokens, D_MODEL), F32),
        compiler_params=pltpu.CompilerParams(
            dimension_semantics=("arbitrary",), vmem_limit_bytes=VMEM_LIMIT_BYTES),
        name="post",
    )(x, *outs, *lses, h, yb, h, h, h, h, wpa, wpb, wo, ln_g, ln_b)


def _rope_tables(seq):
    rows = seq // GRID_W
    axis_dim = HEAD_DIM // 2
    inv_freq = ROPE_THETA ** (-jnp.arange(0, axis_dim, 2, dtype=F32) / axis_dim)
    ang_r = jnp.arange(rows).astype(F32)[:, None] * inv_freq[None]
    ang_c = jnp.arange(GRID_W).astype(F32)[:, None] * inv_freq[None]

    def over_positions(per_row, per_col):
        r = jnp.broadcast_to(per_row[:, None, :], (rows, GRID_W, axis_dim // 2))
        c = jnp.broadcast_to(per_col[None, :, :], (rows, GRID_W, axis_dim // 2))
        return r, c

    cos_r, cos_c = over_positions(jnp.cos(ang_r), jnp.cos(ang_c))
    sin_r, sin_c = over_positions(jnp.sin(ang_r), jnp.sin(ang_c))
    cos = jnp.concatenate([cos_r, cos_c, cos_r, cos_c], axis=-1).reshape(seq, HEAD_DIM)
    sin = jnp.concatenate([-sin_r, -sin_c, sin_r, sin_c], axis=-1).reshape(seq, HEAD_DIM)
    return cos, sin


def _pair_rotary_halves(a, first_col, n_heads):
    seg = a[..., first_col:first_col + n_heads * HEAD_DIM]
    seg = seg.reshape(*seg.shape[:-1], n_heads, 2, 2, HEAD_DIM // 4)
    seg = jnp.swapaxes(seg, -3, -2).reshape(*a.shape[:-1], n_heads * HEAD_DIM)
    return jnp.concatenate([a[..., :first_col], seg, a[..., first_col + n_heads * HEAD_DIM:]], axis=-1)


def _alibi_bias(group, dil):
    slopes = 2.0 ** (-ALIBI_MAX_BIAS * np.arange(1, A_HEADS + 1) / A_HEADS)
    slopes = slopes[group * A_HEADS_PER_GROUP:(group + 1) * A_HEADS_PER_GROUP]
    rel = np.arange(DIL_SUB + 2 * N_SIDE)[None, :] - N_SIDE - np.arange(DIL_SUB)[:, None]
    inside = np.abs(rel) <= N_SIDE
    bias = -slopes[:, None, None] * (dil * np.abs(rel)).astype(np.float32)[None]
    return jnp.asarray(np.where(inside[None], bias, MASK_VALUE), dtype=F32)


def _trunk(x, params, biases, cos, sin):
    batch, seq, _ = x.shape
    xf = x.reshape(batch * seq, D_MODEL)
    w_in, b_in, w_rot, b_rot, q_gain, k_gain, wpa, wpb, wo, ln_g, ln_b = params
    for layer in range(DEPTH):
        h = _in_proj(xf, w_in, b_in, w_rot, b_rot, q_gain, k_gain, cos, sin, seq, layer)
        h3 = h.reshape(batch, seq, IN_WIDTH)
        yb = _gqa_flash(h3).reshape(batch * seq, B_HEADS * HEAD_DIM)
        outs, lses = [], []
        for group, (_, dil) in enumerate(A_GROUPS):
            o, lse = _dilated_group(h3, biases[group], group, dil)
            outs.append(o)
            lses.append(lse)
        xf = _post(xf, outs, lses, h, yb, wpa, wpb, wo, ln_g, ln_b, layer)
    return xf.reshape(batch, seq, D_MODEL)


def _prepare_params(w_in, b_in, q_gain, k_gain, w_proj_a, w_proj_b, w_out, ln_g, ln_b):
    rotary_heads = B_HEADS + B_KV_HEADS
    rotary_cols = slice(T_BQ * COL_TILE, (T_BQ + ROTARY_TILES) * COL_TILE)

    def tile_major(w, tiles):
        return w.astype(BF16).reshape(DEPTH, D_MODEL, tiles, COL_TILE).transpose(0, 2, 1, 3)

    return (
        tile_major(w_in, N_COL_TILES),
        b_in.reshape(DEPTH, 1, IN_WIDTH),
        tile_major(_pair_rotary_halves(w_in[..., rotary_cols], 0, rotary_heads), ROTARY_TILES),
        _pair_rotary_halves(b_in[..., rotary_cols], 0, rotary_heads).reshape(DEPTH, 1, ROTARY_TILES * COL_TILE),
        _pair_rotary_halves(q_gain, 0, 1).reshape(DEPTH, 1, HEAD_DIM),
        _pair_rotary_halves(k_gain, 0, 1).reshape(DEPTH, 1, HEAD_DIM),
        w_proj_a.astype(BF16),
        w_proj_b.astype(BF16),
        w_out.astype(BF16),
        ln_g.reshape(DEPTH, 1, D_MODEL),
        ln_b.reshape(DEPTH, 1, D_MODEL),
    )


def kernel(x_prompt, x_sample, w_in, b_in, q_gain, k_gain, w_proj_a, w_proj_b, w_out, ln_g, ln_b):
    params = _prepare_params(w_in, b_in, q_gain, k_gain, w_proj_a, w_proj_b, w_out, ln_g, ln_b)
    biases = [_alibi_bias(group, dil) for group, (_, dil) in enumerate(A_GROUPS)]
    cos, sin = _rope_tables(max(x_prompt.shape[1], x_sample.shape[1]))
    return (_trunk(x_prompt, params, biases, cos, sin), _trunk(x_sample, params, biases, cos, sin))
```

```python
import functools
import math

import jax
import jax.numpy as jnp
import numpy as np
from jax import lax
from jax.experimental import pallas as pl
from jax.experimental.pallas import tpu as pltpu

F32 = jnp.float32
BF16 = jnp.bfloat16

D_MODEL = 1024
DEPTH = 2
HEAD_DIM = 128
GRID_W = 64
A_GROUPS = ((128, 1), (512, 4), (2048, 16))
A_HEADS_PER_GROUP = 4
A_HEADS = 12
A_WIDTH = 512
ALIBI_MAX_BIAS = 8.0
B_HEADS = 8
B_KV_HEADS = 2
B_GROUP = B_HEADS // B_KV_HEADS
ROPE_THETA = 10000.0
IN_WIDTH = 9728
RMS_EPS = 1e-6
LN_EPS = 1e-5
MASK_VALUE = -1e30
ALPHA = (2 * DEPTH) ** 0.25
QK_SCALE = 1.0 / math.sqrt(HEAD_DIM)
BQ_SCALE = QK_SCALE * math.log2(math.e)
SUBLANES = 8
LOGIT_BOUND = 60.0

COL_TILE = 512
N_COL_TILES = IN_WIDTH // COL_TILE
T_AQ, T_AK, T_AV, T_AG = 0, 3, 6, 9
T_BQ, T_BKV, T_BG = 10, 12, 13
T_GATE_A, T_GATE_B = 15, 17
ROTARY_TILES = T_BKV + 1 - T_BQ
BK_COL128 = T_BKV * COL_TILE // HEAD_DIM
BV_COL128 = BK_COL128 + B_KV_HEADS

N_SIDE = 64

VMEM_LIMIT_BYTES = 56 * 1024 * 1024

SUPER_TILE = 2048
IN_PROJ_CHUNK = 512
GATHER_STRIDE = 4
FLASH_Q_ROWS = 1024
FLASH_K_ROWS = 1024
DIL_SUB = 128
DIL_UNROLL = 8
POST_ROWS = 512
POST_CHUNK = 256


def _sigmoid(x):
    return 0.5 * jnp.tanh(0.5 * x) + 0.5


def _in_proj_kernel(x_ref, w_ref, b_ref, wrot_ref, brot_ref, qg_ref, kg_ref, cos_ref, sin_ref, avg_ref,
                    o_ref, xb_ref, hs_ref, hs2_ref):
    j = pl.program_id(1)
    tm = x_ref.shape[0]

    @pl.when(j == 0)
    def _():
        xb_ref[...] = x_ref[...].astype(BF16)

    lane_tiles = COL_TILE // HEAD_DIM
    n_chunks = tm // IN_PROJ_CHUNK

    def column_tile(epilogue, weights=w_ref, bias=b_ref):
        def run():
            for rc in range(n_chunks):
                rows = slice(rc * IN_PROJ_CHUNK, (rc + 1) * IN_PROJ_CHUNK)
                h = jnp.dot(xb_ref[rows, :], weights[...], preferred_element_type=F32) + bias[...]
                epilogue(rc, rows, h)
        return run

    def head_mean_squares(h):
        width = h.shape[1]
        return jnp.dot((h * h).astype(BF16), avg_ref[0:width, 0:width], preferred_element_type=F32)

    def rms_rope(hh, ms, gain, rows):
        y = hh * lax.rsqrt(ms + RMS_EPS) * gain
        return y * cos_ref[rows, :] + pltpu.roll(y, HEAD_DIM // 2, 1) * sin_ref[rows, :]

    a_scale = jnp.where(j < T_AK, QK_SCALE, 1.0)
    a_group = j % len(A_GROUPS)

    def plain(rc, rows, h):
        o_ref[rows, :] = (h * a_scale).astype(BF16)

    def class_major(dil):
        n = tm // dil
        per_chunk = IN_PROJ_CHUNK // dil

        def epilogue(rc, rows, h):
            for ln in range(lane_tiles):
                hs_ref[rc, ln] = h[:, ln * HEAD_DIM:(ln + 1) * HEAD_DIM] * a_scale
            src_ref, stride = hs_ref, dil
            if dil == GATHER_STRIDE * GATHER_STRIDE:
                quarter = IN_PROJ_CHUNK // GATHER_STRIDE
                for ln in range(lane_tiles):
                    for c4 in range(GATHER_STRIDE):
                        hs2_ref[rc, ln, c4 * quarter:(c4 + 1) * quarter, :] = (
                            hs_ref[rc, ln, pl.ds(c4, quarter, stride=GATHER_STRIDE), :])
                src_ref, stride = hs2_ref, GATHER_STRIDE
            for c in range(dil):
                if src_ref is hs_ref:
                    first = c
                else:
                    first = (c % GATHER_STRIDE) * (IN_PROJ_CHUNK // GATHER_STRIDE) + c // GATHER_STRIDE
                dst = slice(c * n + rc * per_chunk, c * n + (rc + 1) * per_chunk)
                for ln in range(lane_tiles):
                    o_ref[dst, ln * HEAD_DIM:(ln + 1) * HEAD_DIM] = (
                        src_ref[rc, ln, pl.ds(first, per_chunk, stride=stride), :].astype(BF16))
        return epilogue

    def silu(rc, rows, h):
        o_ref[rows, :] = (h * _sigmoid(h)).astype(BF16)

    def rope_q(rc, rows, h):
        ms = head_mean_squares(h)
        for hd in range(lane_tiles):
            sl = slice(hd * HEAD_DIM, (hd + 1) * HEAD_DIM)
            o_ref[rows, sl] = (rms_rope(h[:, sl], ms[:, sl], qg_ref[...], rows) * BQ_SCALE).astype(BF16)

    def rope_k_and_v(rc, rows, h):
        k_width = B_KV_HEADS * HEAD_DIM
        ms = head_mean_squares(h[:, :k_width])
        for hd in range(B_KV_HEADS):
            sl = slice(hd * HEAD_DIM, (hd + 1) * HEAD_DIM)
            o_ref[rows, sl] = rms_rope(h[:, sl], ms[:, sl], kg_ref[...], rows).astype(BF16)
        o_ref[rows, k_width:] = h[:, k_width:].astype(BF16)

    def gate(rc, rows, h):
        o_ref[rows, :] = _sigmoid(h).astype(BF16)

    for group, (_, dil) in enumerate(A_GROUPS):
        pl.when((j < T_AG) & (a_group == group))(column_tile(plain if dil == 1 else class_major(dil)))
    pl.when((j == T_AG) | (j == T_BG) | (j == T_BG + 1))(column_tile(silu))
    pl.when((j == T_BQ) | (j == T_BQ + 1))(column_tile(rope_q, wrot_ref, brot_ref))
    pl.when(j == T_BKV)(column_tile(rope_k_and_v, wrot_ref, brot_ref))
    pl.when(j >= T_GATE_A)(column_tile(gate))


def _in_proj(x, w, b, w_rot, b_rot, q_gain, k_gain, cos, sin, seq, layer):
    tokens = x.shape[0]
    tm = SUPER_TILE
    assert seq % tm == 0
    pos_blocks = seq // tm
    row_spec = lambda width: pl.BlockSpec((tm, width), lambda i, j: (i, 0))
    pos_spec = pl.BlockSpec((tm, HEAD_DIM), lambda i, j: (i % pos_blocks, 0))
    vec_spec = pl.BlockSpec((None, 1, HEAD_DIM), lambda i, j: (layer, 0, 0))
    head_of_col = np.arange(COL_TILE) // HEAD_DIM
    head_average = jnp.asarray((head_of_col[:, None] == head_of_col[None, :]) / HEAD_DIM, dtype=BF16)
    return pl.pallas_call(
        _in_proj_kernel,
        grid=(tokens // tm, N_COL_TILES),
        in_specs=[
            row_spec(D_MODEL),
            pl.BlockSpec((None, None, D_MODEL, COL_TILE), lambda i, j: (layer, j, 0, 0)),
            pl.BlockSpec((None, 1, COL_TILE), lambda i, j: (layer, 0, j)),
            pl.BlockSpec((None, None, D_MODEL, COL_TILE),
                         lambda i, j: (layer, jnp.clip(j - T_BQ, 0, ROTARY_TILES - 1), 0, 0)),
            pl.BlockSpec((None, 1, COL_TILE), lambda i, j: (layer, 0, jnp.clip(j - T_BQ, 0, ROTARY_TILES - 1))),
            vec_spec, vec_spec, pos_spec, pos_spec,
            pl.BlockSpec((COL_TILE, COL_TILE), lambda i, j: (0, 0)),
        ],
        out_specs=pl.BlockSpec((tm, COL_TILE), lambda i, j: (i, j)),
        out_shape=jax.ShapeDtypeStruct((tokens, IN_WIDTH), BF16),
        scratch_shapes=[
            pltpu.VMEM((tm, D_MODEL), BF16),
            pltpu.VMEM((tm // IN_PROJ_CHUNK, COL_TILE // HEAD_DIM, IN_PROJ_CHUNK, HEAD_DIM), F32),
            pltpu.VMEM((tm // IN_PROJ_CHUNK, COL_TILE // HEAD_DIM, IN_PROJ_CHUNK, HEAD_DIM), F32),
        ],
        compiler_params=pltpu.CompilerParams(
            dimension_semantics=("arbitrary", "arbitrary"), vmem_limit_bytes=VMEM_LIMIT_BYTES),
        name="in_proj",
    )(x, w, b, w_rot, b_rot, q_gain, k_gain, cos, sin, head_average)


def _flash_kernel(q_ref, k_ref, v_ref, g_ref, o_ref, qt_ref, vt_ref, m_ref, l_ref, acc_ref, k2max_ref,
                  *, tq, tk, n_chunks):
    i = pl.program_id(2)

    @pl.when(i == 0)
    def _():
        def fill(c, k2max):
            start = pl.multiple_of(c * tk, tk)
            vt_ref[c] = v_ref[pl.ds(start, tk), :].astype(F32).T.astype(BF16)
            kf = k_ref[pl.ds(start, tk), :].astype(F32)
            k2 = jnp.max(jnp.sum(kf * kf, axis=-1, keepdims=True), axis=0, keepdims=True)
            return jnp.maximum(k2max, k2)

        k2max_ref[...] = lax.fori_loop(0, n_chunks, fill, jnp.zeros((1, 1), F32))

    q2 = jnp.zeros((1, 1), F32)
    for hd in range(B_GROUP):
        qf = q_ref[:, hd * HEAD_DIM:(hd + 1) * HEAD_DIM].astype(F32).T
        qt_ref[:, hd * tq:(hd + 1) * tq] = qf.astype(BF16)
        q2 = jnp.maximum(q2, jnp.max(jnp.sum(qf * qf, axis=0, keepdims=True), axis=1, keepdims=True))
    acc_ref[...] = jnp.zeros(acc_ref.shape, F32)
    l_ref[...] = jnp.zeros(l_ref.shape, F32)

    logits_bounded = (q2 * k2max_ref[...])[0, 0] <= LOGIT_BOUND * LOGIT_BOUND

    def scores(c):
        start = pl.multiple_of(c * tk, tk)
        return jnp.dot(k_ref[pl.ds(start, tk), :], qt_ref[...], preferred_element_type=F32)

    def sublane_partial_sums(p):
        return jnp.sum(p.reshape(tk // SUBLANES, SUBLANES, p.shape[1]), axis=0)

    def bounded_body(c, carry):
        p = jnp.exp2(scores(c))
        l_ref[...] += sublane_partial_sums(p)
        acc_ref[...] += jnp.dot(vt_ref[c], p.astype(BF16), preferred_element_type=F32)
        return carry

    def online_body(c, carry):
        st = scores(c)
        m_prev = m_ref[...]
        m_new = jnp.maximum(m_prev, jnp.max(st, axis=0, keepdims=True))
        alpha = jnp.exp2(m_prev - m_new)
        p = jnp.exp2(st - m_new)
        l_ref[...] = alpha * l_ref[...] + sublane_partial_sums(p)
        acc_ref[...] = alpha * acc_ref[...] + jnp.dot(vt_ref[c], p.astype(BF16), preferred_element_type=F32)
        m_ref[...] = m_new
        return carry

    @pl.when(logits_bounded)
    def _():
        lax.fori_loop(0, n_chunks, bounded_body, 0, unroll=2)

    @pl.when(jnp.logical_not(logits_bounded))
    def _():
        m_ref[...] = jnp.full(m_ref.shape, MASK_VALUE, F32)
        lax.fori_loop(0, n_chunks, online_body, 0, unroll=2)

    ot = acc_ref[...] / jnp.sum(l_ref[...], axis=0, keepdims=True)
    for hd in range(B_GROUP):
        sl = slice(hd * HEAD_DIM, (hd + 1) * HEAD_DIM)
        o_ref[:, sl] = (ot[:, hd * tq:(hd + 1) * tq].T * g_ref[:, sl].astype(F32)).astype(BF16)


def _gqa_flash(h3):
    batch, seq, _ = h3.shape
    tq, tk = FLASH_Q_ROWS, FLASH_K_ROWS
    n_chunks = seq // tk
    kernel = functools.partial(_flash_kernel, tq=tq, tk=tk, n_chunks=n_chunks)
    return pl.pallas_call(
        kernel,
        grid=(batch, B_KV_HEADS, seq // tq),
        in_specs=[
            pl.BlockSpec((None, tq, COL_TILE), lambda b, g, i: (b, i, T_BQ + g)),
            pl.BlockSpec((None, seq, HEAD_DIM), lambda b, g, i: (b, 0, BK_COL128 + g)),
            pl.BlockSpec((None, seq, HEAD_DIM), lambda b, g, i: (b, 0, BV_COL128 + g)),
            pl.BlockSpec((None, tq, COL_TILE), lambda b, g, i: (b, i, T_BG + g)),
        ],
        out_specs=pl.BlockSpec((None, tq, COL_TILE), lambda b, g, i: (b, i, g)),
        out_shape=jax.ShapeDtypeStruct((batch, seq, B_HEADS * HEAD_DIM), BF16),
        scratch_shapes=[
            pltpu.VMEM((HEAD_DIM, B_GROUP * tq), BF16),
            pltpu.VMEM((n_chunks, HEAD_DIM, tk), BF16),
            pltpu.VMEM((1, B_GROUP * tq), F32),
            pltpu.VMEM((SUBLANES, B_GROUP * tq), F32),
            pltpu.VMEM((HEAD_DIM, B_GROUP * tq), F32),
            pltpu.VMEM((1, 1), F32),
        ],
        compiler_params=pltpu.CompilerParams(
            dimension_semantics=("arbitrary", "arbitrary", "arbitrary"), vmem_limit_bytes=VMEM_LIMIT_BYTES),
        name="gqa_flash",
    )(h3, h3, h3, h3)


def _dilated_kernel(q_ref, k_ref, v_ref, kp_ref, kn_ref, vp_ref, vn_ref, bias_ref, o_ref, lse_ref,
                    kw_ref, vw_ref, stage_ref, *, dil, length):
    n = SUPER_TILE // dil
    win = DIL_SUB + 2 * N_SIDE
    first_pos = pl.program_id(1) * n
    lanes_per_head = HEAD_DIM // A_HEADS_PER_GROUP

    def sub_block(c, a):
        r0 = pl.multiple_of(a * DIL_SUB, DIL_SUB)
        key_pos = first_pos + a * DIL_SUB - N_SIDE + lax.broadcasted_iota(jnp.int32, (1, win), 1)
        key_ok = (key_pos >= 0) & (key_pos < length)
        token_rows = pl.ds(r0, DIL_SUB) if dil == 1 else pl.ds(c + dil * r0, DIL_SUB, stride=dil)
        lse_cols = []
        for hd in range(A_HEADS_PER_GROUP):
            sl = slice(hd * HEAD_DIM, (hd + 1) * HEAD_DIM)
            q = q_ref[pl.ds(pl.multiple_of(c * n + r0, DIL_SUB), DIL_SUB), sl]
            k = kw_ref[c, pl.ds(r0, win), sl]
            v = vw_ref[c, pl.ds(r0, win), sl]
            s = lax.dot_general(q, k, (((1,), (1,)), ((), ())), preferred_element_type=F32) + bias_ref[hd]
            s = jnp.where(key_ok, s, MASK_VALUE)
            m = jnp.max(s, axis=-1, keepdims=True)
            p = jnp.exp(s - m)
            l = jnp.sum(p, axis=-1, keepdims=True)
            o = jnp.dot(p.astype(BF16), v, preferred_element_type=F32) / l
            if dil == 1:
                o_ref[token_rows, sl] = o.astype(BF16)
            else:
                stage_ref[hd, token_rows, :] = o
            lse_cols.append(jnp.broadcast_to(m + jnp.log(l), (DIL_SUB, lanes_per_head)))
        lse_ref[token_rows, :] = jnp.concatenate(lse_cols, axis=-1)

    for c in range(dil):
        rows = slice(c * n, (c + 1) * n)
        kw_ref[c, 0:N_SIDE, :] = kp_ref[c]
        kw_ref[c, N_SIDE:N_SIDE + n, :] = k_ref[rows, :]
        kw_ref[c, N_SIDE + n:, :] = kn_ref[c]
        vw_ref[c, 0:N_SIDE, :] = vp_ref[c]
        vw_ref[c, N_SIDE:N_SIDE + n, :] = v_ref[rows, :]
        vw_ref[c, N_SIDE + n:, :] = vn_ref[c]

    subs_per_class = n // DIL_SUB

    def one_sub_block(sb, carry):
        sub_block(sb // subs_per_class, sb % subs_per_class)
        return carry

    lax.fori_loop(0, SUPER_TILE // DIL_SUB, one_sub_block, 0, unroll=DIL_UNROLL)

    if dil > 1:
        for hd in range(A_HEADS_PER_GROUP):
            o_ref[:, hd * HEAD_DIM:(hd + 1) * HEAD_DIM] = stage_ref[hd].astype(BF16)


def _dilated_group(h3, bias, group, dil):
    batch, seq, _ = h3.shape
    tiles = seq // SUPER_TILE
    n = SUPER_TILE // dil
    halos = n // N_SIDE
    h6 = h3.reshape(batch, tiles, dil, halos, N_SIDE, IN_WIDTH)

    def cur(tile):
        return pl.BlockSpec((None, SUPER_TILE, COL_TILE), lambda b, t: (b, t, tile + group))

    def prev(tile):
        return pl.BlockSpec((None, None, dil, None, N_SIDE, COL_TILE),
                            lambda b, t: (b, jnp.maximum(t - 1, 0), 0, halos - 1, 0, tile + group))

    def nxt(tile):
        return pl.BlockSpec((None, None, dil, None, N_SIDE, COL_TILE),
                            lambda b, t: (b, jnp.minimum(t + 1, tiles - 1), 0, 0, 0, tile + group))

    kernel = functools.partial(_dilated_kernel, dil=dil, length=seq // dil)
    o, lse = pl.pallas_call(
        kernel,
        grid=(batch, tiles),
        in_specs=[
            cur(T_AQ), cur(T_AK), cur(T_AV), prev(T_AK), nxt(T_AK), prev(T_AV), nxt(T_AV),
            pl.BlockSpec((A_HEADS_PER_GROUP, DIL_SUB, DIL_SUB + 2 * N_SIDE), lambda b, t: (0, 0, 0)),
        ],
        out_specs=[
            pl.BlockSpec((None, SUPER_TILE, A_WIDTH), lambda b, t: (b, t, 0)),
            pl.BlockSpec((None, SUPER_TILE, HEAD_DIM), lambda b, t: (b, t, 0)),
        ],
        out_shape=[
            jax.ShapeDtypeStruct((batch, seq, A_WIDTH), BF16),
            jax.ShapeDtypeStruct((batch, seq, HEAD_DIM), F32),
        ],
        scratch_shapes=[
            pltpu.VMEM((dil, n + 2 * N_SIDE, COL_TILE), BF16),
            pltpu.VMEM((dil, n + 2 * N_SIDE, COL_TILE), BF16),
            pltpu.VMEM((A_HEADS_PER_GROUP, SUPER_TILE, HEAD_DIM), F32),
        ],
        compiler_params=pltpu.CompilerParams(
            dimension_semantics=("arbitrary", "arbitrary"), vmem_limit_bytes=VMEM_LIMIT_BYTES),
        name=f"dilated_g{group}",
    )(h3, h3, h3, h6, h6, h6, h6, bias)
    return o.reshape(batch * seq, A_WIDTH), lse.reshape(batch * seq, HEAD_DIM)


def _post_kernel(x_ref, o0_ref, o1_ref, o2_ref, l0_ref, l1_ref, l2_ref, ag_ref, yb_ref,
                 ga0_ref, ga1_ref, gb0_ref, gb1_ref, wpa_ref, wpb_ref, wo_ref, lng_ref, lnb_ref, y_ref):
    lanes_per_head = HEAD_DIM // A_HEADS_PER_GROUP
    for rc in range(x_ref.shape[0] // POST_CHUNK):
        rows = slice(rc * POST_CHUNK, (rc + 1) * POST_CHUNK)
        l0, l1, l2 = l0_ref[rows, :], l1_ref[rows, :], l2_ref[rows, :]
        top = jnp.maximum(jnp.maximum(l0, l1), l2)
        e0, e1, e2 = jnp.exp(l0 - top), jnp.exp(l1 - top), jnp.exp(l2 - top)
        inv = 1.0 / (e0 + e1 + e2)
        ya_heads = []
        for hd in range(A_HEADS_PER_GROUP):
            sl = slice(hd * HEAD_DIM, (hd + 1) * HEAD_DIM)
            pick = slice(hd * lanes_per_head, hd * lanes_per_head + 1)
            w0 = jnp.broadcast_to((e0 * inv)[:, pick], (POST_CHUNK, HEAD_DIM))
            w1 = jnp.broadcast_to((e1 * inv)[:, pick], (POST_CHUNK, HEAD_DIM))
            w2 = jnp.broadcast_to((e2 * inv)[:, pick], (POST_CHUNK, HEAD_DIM))
            mix = (w0 * o0_ref[rows, sl].astype(F32) + w1 * o1_ref[rows, sl].astype(F32)
                   + w2 * o2_ref[rows, sl].astype(F32))
            ya_heads.append((mix * ag_ref[rows, sl].astype(F32)).astype(BF16))
        ya = jnp.concatenate(ya_heads, axis=-1)
        pa = jnp.dot(ya, wpa_ref[...], preferred_element_type=F32)
        pb = jnp.dot(yb_ref[rows, :], wpb_ref[...], preferred_element_type=F32)
        gate_a = jnp.concatenate([ga0_ref[rows, :], ga1_ref[rows, :]], axis=-1).astype(F32)
        gate_b = jnp.concatenate([gb0_ref[rows, :], gb1_ref[rows, :]], axis=-1).astype(F32)
        merged = (gate_a * pa + gate_b * pb).astype(BF16)
        z = ALPHA * x_ref[rows, :] + jnp.dot(merged, wo_ref[...], preferred_element_type=F32)
        mu = jnp.mean(z, axis=-1, keepdims=True)
        zc = z - mu
        var = jnp.mean(zc * zc, axis=-1, keepdims=True)
        y_ref[rows, :] = zc * lax.rsqrt(var + LN_EPS) * lng_ref[...] + lnb_ref[...]


def _post(x, outs, lses, h, yb, wpa, wpb, wo, ln_g, ln_b, layer):
    tokens = x.shape[0]
    tm = POST_ROWS
    row = lambda width, col=0: pl.BlockSpec((tm, width), lambda i, col=col: (i, col))
    full = lambda a: pl.BlockSpec((None,) + a.shape[1:], lambda i: (layer, 0, 0))
    return pl.pallas_call(
        _post_kernel,
        grid=(tokens // tm,),
        in_specs=[
            row(D_MODEL),
            row(A_WIDTH), row(A_WIDTH), row(A_WIDTH),
            row(HEAD_DIM), row(HEAD_DIM), row(HEAD_DIM),
            row(COL_TILE, T_AG),
            row(B_HEADS * HEAD_DIM),
            row(COL_TILE, T_GATE_A), row(COL_TILE, T_GATE_A + 1),
            row(COL_TILE, T_GATE_B), row(COL_TILE, T_GATE_B + 1),
            full(wpa), full(wpb), full(wo), full(ln_g), full(ln_b),
        ],
        out_specs=row(D_MODEL),
        out_shape=jax.ShapeDtypeStruct((tokens, D_MODEL), F32),
        compiler_params=pltpu.CompilerParams(
            dimension_semantics=("arbitrary",), vmem_limit_bytes=VMEM_LIMIT_BYTES),
        name="post",
    )(x, *outs, *lses, h, yb, h, h, h, h, wpa, wpb, wo, ln_g, ln_b)


def _rope_tables(seq):
    rows = seq // GRID_W
    axis_dim = HEAD_DIM // 2
    inv_freq = ROPE_THETA ** (-jnp.arange(0, axis_dim, 2, dtype=F32) / axis_dim)
    ang_r = jnp.arange(rows).astype(F32)[:, None] * inv_freq[None]
    ang_c = jnp.arange(GRID_W).astype(F32)[:, None] * inv_freq[None]

    def over_positions(per_row, per_col):
        r = jnp.broadcast_to(per_row[:, None, :], (rows, GRID_W, axis_dim // 2))
        c = jnp.broadcast_to(per_col[None, :, :], (rows, GRID_W, axis_dim // 2))
        return r, c

    cos_r, cos_c = over_positions(jnp.cos(ang_r), jnp.cos(ang_c))
    sin_r, sin_c = over_positions(jnp.sin(ang_r), jnp.sin(ang_c))
    cos = jnp.concatenate([cos_r, cos_c, cos_r, cos_c], axis=-1).reshape(seq, HEAD_DIM)
    sin = jnp.concatenate([-sin_r, -sin_c, sin_r, sin_c], axis=-1).reshape(seq, HEAD_DIM)
    return cos, sin


def _pair_rotary_halves(a, first_col, n_heads):
    seg = a[..., first_col:first_col + n_heads * HEAD_DIM]
    seg = seg.reshape(*seg.shape[:-1], n_heads, 2, 2, HEAD_DIM // 4)
    seg = jnp.swapaxes(seg, -3, -2).reshape(*a.shape[:-1], n_heads * HEAD_DIM)
    return jnp.concatenate([a[..., :first_col], seg, a[..., first_col + n_heads * HEAD_DIM:]], axis=-1)


def _alibi_bias(group, dil):
    slopes = 2.0 ** (-ALIBI_MAX_BIAS * np.arange(1, A_HEADS + 1) / A_HEADS)
    slopes = slopes[group * A_HEADS_PER_GROUP:(group + 1) * A_HEADS_PER_GROUP]
    rel = np.arange(DIL_SUB + 2 * N_SIDE)[None, :] - N_SIDE - np.arange(DIL_SUB)[:, None]
    inside = np.abs(rel) <= N_SIDE
    bias = -slopes[:, None, None] * (dil * np.abs(rel)).astype(np.float32)[None]
    return jnp.asarray(np.where(inside[None], bias, MASK_VALUE), dtype=F32)


def _trunk(x, params, biases, cos, sin):
    batch, seq, _ = x.shape
    xf = x.reshape(batch * seq, D_MODEL)
    w_in, b_in, w_rot, b_rot, q_gain, k_gain, wpa, wpb, wo, ln_g, ln_b = params
    for layer in range(DEPTH):
        h = _in_proj(xf, w_in, b_in, w_rot, b_rot, q_gain, k_gain, cos, sin, seq, layer)
        h3 = h.reshape(batch, seq, IN_WIDTH)
        yb = _gqa_flash(h3).reshape(batch * seq, B_HEADS * HEAD_DIM)
        outs, lses = [], []
        for group, (_, dil) in enumerate(A_GROUPS):
            o, lse = _dilated_group(h3, biases[group], group, dil)
            outs.append(o)
            lses.append(lse)
        xf = _post(xf, outs, lses, h, yb, wpa, wpb, wo, ln_g, ln_b, layer)
    return xf.reshape(batch, seq, D_MODEL)


def _prepare_params(w_in, b_in, q_gain, k_gain, w_proj_a, w_proj_b, w_out, ln_g, ln_b):
    rotary_heads = B_HEADS + B_KV_HEADS
    rotary_cols = slice(T_BQ * COL_TILE, (T_BQ + ROTARY_TILES) * COL_TILE)

    def tile_major(w, tiles):
        return w.astype(BF16).reshape(DEPTH, D_MODEL, tiles, COL_TILE).transpose(0, 2, 1, 3)

    return (
        tile_major(w_in, N_COL_TILES),
        b_in.reshape(DEPTH, 1, IN_WIDTH),
        tile_major(_pair_rotary_halves(w_in[..., rotary_cols], 0, rotary_heads), ROTARY_TILES),
        _pair_rotary_halves(b_in[..., rotary_cols], 0, rotary_heads).reshape(DEPTH, 1, ROTARY_TILES * COL_TILE),
        _pair_rotary_halves(q_gain, 0, 1).reshape(DEPTH, 1, HEAD_DIM),
        _pair_rotary_halves(k_gain, 0, 1).reshape(DEPTH, 1, HEAD_DIM),
        w_proj_a.astype(BF16),
        w_proj_b.astype(BF16),
        w_out.astype(BF16),
        ln_g.reshape(DEPTH, 1, D_MODEL),
        ln_b.reshape(DEPTH, 1, D_MODEL),
    )


def kernel(x_prompt, x_sample, w_in, b_in, q_gain, k_gain, w_proj_a, w_proj_b, w_out, ln_g, ln_b):
    params = _prepare_params(w_in, b_in, q_gain, k_gain, w_proj_a, w_proj_b, w_out, ln_g, ln_b)
    biases = [_alibi_bias(group, dil) for group, (_, dil) in enumerate(A_GROUPS)]
    cos, sin = _rope_tables(max(x_prompt.shape[1], x_sample.shape[1]))
    return (_trunk(x_prompt, params, biases, cos, sin), _trunk(x_sample, params, biases, cos, sin))
```

```python
import functools
import math

import jax
import jax.numpy as jnp
import numpy as np
from jax import lax
from jax.experimental import pallas as pl
from jax.experimental.pallas import tpu as pltpu

F32 = jnp.float32
BF16 = jnp.bfloat16

D_MODEL = 1024
DEPTH = 2
HEAD_DIM = 128
GRID_W = 64
A_GROUPS = ((128, 1), (512, 4), (2048, 16))
A_HEADS_PER_GROUP = 4
A_HEADS = 12
A_WIDTH = 512
ALIBI_MAX_BIAS = 8.0
B_HEADS = 8
B_KV_HEADS = 2
B_GROUP = B_HEADS // B_KV_HEADS
ROPE_THETA = 10000.0
IN_WIDTH = 9728
RMS_EPS = 1e-6
LN_EPS = 1e-5
MASK_VALUE = -1e30
ALPHA = (2 * DEPTH) ** 0.25
QK_SCALE = 1.0 / math.sqrt(HEAD_DIM)
BQ_SCALE = QK_SCALE * math.log2(math.e)
SUBLANES = 8
LOGIT_BOUND = 60.0

COL_TILE = 512
N_COL_TILES = IN_WIDTH // COL_TILE
T_AQ, T_AK, T_AV, T_AG = 0, 3, 6, 9
T_BQ, T_BKV, T_BG = 10, 12, 13
T_GATE_A, T_GATE_B = 15, 17
ROTARY_TILES = T_BKV + 1 - T_BQ
BK_COL128 = T_BKV * COL_TILE // HEAD_DIM
BV_COL128 = BK_COL128 + B_KV_HEADS

N_SIDE = 64

VMEM_LIMIT_BYTES = 56 * 1024 * 1024

SUPER_TILE = 2048
IN_PROJ_CHUNK = 512
GATHER_STRIDE = 4
FLASH_Q_ROWS_CHOICES = (1024, 512)
FLASH_SCORE_ELEMS = 4 * 1024 * 1024
DIL_SUB = 128
DIL_UNROLL = 8
POST_ROWS = 1024
POST_CHUNK = 512


def _sigmoid(x):
    return 0.5 * jnp.tanh(0.5 * x) + 0.5


def _in_proj_kernel(x_ref, w_ref, b_ref, wrot_ref, brot_ref, qg_ref, kg_ref, cos_ref, sin_ref, avg_ref,
                    o_ref, xb_ref, hs_ref, hs2_ref):
    j = pl.program_id(1)
    tm = x_ref.shape[0]

    @pl.when(j == 0)
    def _():
        xb_ref[...] = x_ref[...].astype(BF16)

    lane_tiles = COL_TILE // HEAD_DIM
    n_chunks = tm // IN_PROJ_CHUNK

    def column_tile(epilogue, weights=w_ref, bias=b_ref):
        def run():
            for rc in range(n_chunks):
                rows = slice(rc * IN_PROJ_CHUNK, (rc + 1) * IN_PROJ_CHUNK)
                h = jnp.dot(xb_ref[rows, :], weights[...], preferred_element_type=F32) + bias[...]
                epilogue(rc, rows, h)
        return run

    def head_mean_squares(h):
        width = h.shape[1]
        return jnp.dot((h * h).astype(BF16), avg_ref[0:width, 0:width], preferred_element_type=F32)

    def rms_rope(hh, ms, gain, rows):
        y = hh * lax.rsqrt(ms + RMS_EPS) * gain
        return y * cos_ref[rows, :] + pltpu.roll(y, HEAD_DIM // 2, 1) * sin_ref[rows, :]

    a_scale = jnp.where(j < T_AK, QK_SCALE, 1.0)
    a_group = j % len(A_GROUPS)

    def plain(rc, rows, h):
        o_ref[rows, :] = (h * a_scale).astype(BF16)

    def class_major(dil):
        n = tm // dil
        per_chunk = IN_PROJ_CHUNK // dil

        def epilogue(rc, rows, h):
            for ln in range(lane_tiles):
                hs_ref[rc, ln] = h[:, ln * HEAD_DIM:(ln + 1) * HEAD_DIM] * a_scale
            src_ref, stride = hs_ref, dil
            if dil == GATHER_STRIDE * GATHER_STRIDE:
                quarter = IN_PROJ_CHUNK // GATHER_STRIDE
                for ln in range(lane_tiles):
                    for c4 in range(GATHER_STRIDE):
                        hs2_ref[rc, ln, c4 * quarter:(c4 + 1) * quarter, :] = (
                            hs_ref[rc, ln, pl.ds(c4, quarter, stride=GATHER_STRIDE), :])
                src_ref, stride = hs2_ref, GATHER_STRIDE
            for c in range(dil):
                if src_ref is hs_ref:
                    first = c
                else:
                    first = (c % GATHER_STRIDE) * (IN_PROJ_CHUNK // GATHER_STRIDE) + c // GATHER_STRIDE
                dst = slice(c * n + rc * per_chunk, c * n + (rc + 1) * per_chunk)
                for ln in range(lane_tiles):
                    o_ref[dst, ln * HEAD_DIM:(ln + 1) * HEAD_DIM] = (
                        src_ref[rc, ln, pl.ds(first, per_chunk, stride=stride), :].astype(BF16))
        return epilogue

    def silu(rc, rows, h):
        o_ref[rows, :] = (h * _sigmoid(h)).astype(BF16)

    def rope_q(rc, rows, h):
        ms = head_mean_squares(h)
        for hd in range(lane_tiles):
            sl = slice(hd * HEAD_DIM, (hd + 1) * HEAD_DIM)
            o_ref[rows, sl] = (rms_rope(h[:, sl], ms[:, sl], qg_ref[...], rows) * BQ_SCALE).astype(BF16)

    def rope_k_and_v(rc, rows, h):
        k_width = B_KV_HEADS * HEAD_DIM
        ms = head_mean_squares(h[:, :k_width])
        for hd in range(B_KV_HEADS):
            sl = slice(hd * HEAD_DIM, (hd + 1) * HEAD_DIM)
            o_ref[rows, sl] = rms_rope(h[:, sl], ms[:, sl], kg_ref[...], rows).astype(BF16)
        o_ref[rows, k_width:] = h[:, k_width:].astype(BF16)

    def gate(rc, rows, h):
        o_ref[rows, :] = _sigmoid(h).astype(BF16)

    for group, (_, dil) in enumerate(A_GROUPS):
        pl.when((j < T_AG) & (a_group == group))(column_tile(plain if dil == 1 else class_major(dil)))
    pl.when((j == T_AG) | (j == T_BG) | (j == T_BG + 1))(column_tile(silu))
    pl.when((j == T_BQ) | (j == T_BQ + 1))(column_tile(rope_q, wrot_ref, brot_ref))
    pl.when(j == T_BKV)(column_tile(rope_k_and_v, wrot_ref, brot_ref))
    pl.when(j >= T_GATE_A)(column_tile(gate))


def _in_proj(x, w, b, w_rot, b_rot, q_gain, k_gain, cos, sin, seq, layer):
    tokens = x.shape[0]
    tm = SUPER_TILE
    assert seq % tm == 0
    pos_blocks = seq // tm
    row_spec = lambda width: pl.BlockSpec((tm, width), lambda i, j: (i, 0))
    pos_spec = pl.BlockSpec((tm, HEAD_DIM), lambda i, j: (i % pos_blocks, 0))
    vec_spec = pl.BlockSpec((None, 1, HEAD_DIM), lambda i, j: (layer, 0, 0))
    head_of_col = np.arange(COL_TILE) // HEAD_DIM
    head_average = jnp.asarray((head_of_col[:, None] == head_of_col[None, :]) / HEAD_DIM, dtype=BF16)
    return pl.pallas_call(
        _in_proj_kernel,
        grid=(tokens // tm, N_COL_TILES),
        in_specs=[
            row_spec(D_MODEL),
            pl.BlockSpec((None, None, D_MODEL, COL_TILE), lambda i, j: (layer, j, 0, 0)),
            pl.BlockSpec((None, 1, COL_TILE), lambda i, j: (layer, 0, j)),
            pl.BlockSpec((None, None, D_MODEL, COL_TILE),
                         lambda i, j: (layer, jnp.clip(j - T_BQ, 0, ROTARY_TILES - 1), 0, 0)),
            pl.BlockSpec((None, 1, COL_TILE), lambda i, j: (layer, 0, jnp.clip(j - T_BQ, 0, ROTARY_TILES - 1))),
            vec_spec, vec_spec, pos_spec, pos_spec,
            pl.BlockSpec((COL_TILE, COL_TILE), lambda i, j: (0, 0)),
        ],
        out_specs=pl.BlockSpec((tm, COL_TILE), lambda i, j: (i, j)),
        out_shape=jax.ShapeDtypeStruct((tokens, IN_WIDTH), BF16),
        scratch_shapes=[
            pltpu.VMEM((tm, D_MODEL), BF16),
            pltpu.VMEM((tm // IN_PROJ_CHUNK, COL_TILE // HEAD_DIM, IN_PROJ_CHUNK, HEAD_DIM), F32),
            pltpu.VMEM((tm // IN_PROJ_CHUNK, COL_TILE // HEAD_DIM, IN_PROJ_CHUNK, HEAD_DIM), F32),
        ],
        compiler_params=pltpu.CompilerParams(
            dimension_semantics=("arbitrary", "arbitrary"), vmem_limit_bytes=VMEM_LIMIT_BYTES),
        name="in_proj",
    )(x, w, b, w_rot, b_rot, q_gain, k_gain, cos, sin, head_average)


def _flash_kernel(q_ref, k_ref, v_ref, g_ref, o_ref, qt_ref, vt_ref, m_ref, l_ref, acc_ref, k2max_ref,
                  *, tq, tk, n_chunks):
    i = pl.program_id(2)

    @pl.when(i == 0)
    def _():
        def fill(c, k2max):
            start = pl.multiple_of(c * tk, tk)
            vt_ref[c] = v_ref[pl.ds(start, tk), :].astype(F32).T.astype(BF16)
            kf = k_ref[pl.ds(start, tk), :].astype(F32)
            k2 = jnp.max(jnp.sum(kf * kf, axis=-1, keepdims=True), axis=0, keepdims=True)
            return jnp.maximum(k2max, k2)

        k2max_ref[...] = lax.fori_loop(0, n_chunks, fill, jnp.zeros((1, 1), F32))

    q2 = jnp.zeros((1, 1), F32)
    for hd in range(B_GROUP):
        qf = q_ref[:, hd * HEAD_DIM:(hd + 1) * HEAD_DIM].astype(F32).T
        qt_ref[:, hd * tq:(hd + 1) * tq] = qf.astype(BF16)
        q2 = jnp.maximum(q2, jnp.max(jnp.sum(qf * qf, axis=0, keepdims=True), axis=1, keepdims=True))
    acc_ref[...] = jnp.zeros(acc_ref.shape, F32)
    l_ref[...] = jnp.zeros(l_ref.shape, F32)

    logits_bounded = (q2 * k2max_ref[...])[0, 0] <= LOGIT_BOUND * LOGIT_BOUND

    def scores(c):
        start = pl.multiple_of(c * tk, tk)
        return jnp.dot(k_ref[pl.ds(start, tk), :], qt_ref[...], preferred_element_type=F32)

    def sublane_partial_sums(p):
        return jnp.sum(p.reshape(tk // SUBLANES, SUBLANES, p.shape[1]), axis=0)

    def bounded_body(c, carry):
        p = jnp.exp2(scores(c))
        l_ref[...] += sublane_partial_sums(p)
        acc_ref[...] += jnp.dot(vt_ref[c], p.astype(BF16), preferred_element_type=F32)
        return carry

    def online_body(c, carry):
        st = scores(c)
        m_prev = m_ref[...]
        m_new = jnp.maximum(m_prev, jnp.max(st, axis=0, keepdims=True))
        alpha = jnp.exp2(m_prev - m_new)
        p = jnp.exp2(st - m_new)
        l_ref[...] = alpha * l_ref[...] + sublane_partial_sums(p)
        acc_ref[...] = alpha * acc_ref[...] + jnp.dot(vt_ref[c], p.astype(BF16), preferred_element_type=F32)
        m_ref[...] = m_new
        return carry

    @pl.when(logits_bounded)
    def _():
        lax.fori_loop(0, n_chunks, bounded_body, 0, unroll=2)

    @pl.when(jnp.logical_not(logits_bounded))
    def _():
        m_ref[...] = jnp.full(m_ref.shape, MASK_VALUE, F32)
        lax.fori_loop(0, n_chunks, online_body, 0, unroll=2)

    ot = acc_ref[...] / jnp.sum(l_ref[...], axis=0, keepdims=True)
    for hd in range(B_GROUP):
        sl = slice(hd * HEAD_DIM, (hd + 1) * HEAD_DIM)
        o_ref[:, sl] = (ot[:, hd * tq:(hd + 1) * tq].T * g_ref[:, sl].astype(F32)).astype(BF16)


def _flash_tiles(seq):
    for tq in FLASH_Q_ROWS_CHOICES:
        tk = FLASH_SCORE_ELEMS // (B_GROUP * tq)
        kv_bytes = 2 * 2 * seq * HEAD_DIM * 2 + seq * HEAD_DIM * 2
        block_bytes = 3 * 2 * tq * COL_TILE * 2
        state_bytes = B_GROUP * tq * (HEAD_DIM * (2 + 4) + SUBLANES * 4 + 4)
        score_bytes = FLASH_SCORE_ELEMS * (4 + 2)
        if seq % tq == 0 and kv_bytes + block_bytes + state_bytes + score_bytes <= VMEM_LIMIT_BYTES:
            return tq, tk
    raise ValueError(f"no flash tiling fits VMEM for sequence length {seq}")


def _gqa_flash(h3):
    batch, seq, _ = h3.shape
    tq, tk = _flash_tiles(seq)
    n_chunks = seq // tk
    kernel = functools.partial(_flash_kernel, tq=tq, tk=tk, n_chunks=n_chunks)
    return pl.pallas_call(
        kernel,
        grid=(batch, B_KV_HEADS, seq // tq),
        in_specs=[
            pl.BlockSpec((None, tq, COL_TILE), lambda b, g, i: (b, i, T_BQ + g)),
            pl.BlockSpec((None, seq, HEAD_DIM), lambda b, g, i: (b, 0, BK_COL128 + g)),
            pl.BlockSpec((None, seq, HEAD_DIM), lambda b, g, i: (b, 0, BV_COL128 + g)),
            pl.BlockSpec((None, tq, COL_TILE), lambda b, g, i: (b, i, T_BG + g)),
        ],
        out_specs=pl.BlockSpec((None, tq, COL_TILE), lambda b, g, i: (b, i, g)),
        out_shape=jax.ShapeDtypeStruct((batch, seq, B_HEADS * HEAD_DIM), BF16),
        scratch_shapes=[
            pltpu.VMEM((HEAD_DIM, B_GROUP * tq), BF16),
            pltpu.VMEM((n_chunks, HEAD_DIM, tk), BF16),
            pltpu.VMEM((1, B_GROUP * tq), F32),
            pltpu.VMEM((SUBLANES, B_GROUP * tq), F32),
            pltpu.VMEM((HEAD_DIM, B_GROUP * tq), F32),
            pltpu.VMEM((1, 1), F32),
        ],
        compiler_params=pltpu.CompilerParams(
            dimension_semantics=("arbitrary", "arbitrary", "arbitrary"), vmem_limit_bytes=VMEM_LIMIT_BYTES),
        name="gqa_flash",
    )(h3, h3, h3, h3)


def _dilated_kernel(q_ref, k_ref, v_ref, kp_ref, kn_ref, vp_ref, vn_ref, bias_ref, o_ref, lse_ref,
                    kw_ref, vw_ref, stage_ref, *, dil, length):
    n = SUPER_TILE // dil
    win = DIL_SUB + 2 * N_SIDE
    first_pos = pl.program_id(1) * n
    lanes_per_head = HEAD_DIM // A_HEADS_PER_GROUP

    def sub_block(c, a):
        r0 = pl.multiple_of(a * DIL_SUB, DIL_SUB)
        key_pos = first_pos + a * DIL_SUB - N_SIDE + lax.broadcasted_iota(jnp.int32, (1, win), 1)
        key_ok = (key_pos >= 0) & (key_pos < length)
        token_rows = pl.ds(r0, DIL_SUB) if dil == 1 else pl.ds(c + dil * r0, DIL_SUB, stride=dil)
        lse_cols = []
        for hd in range(A_HEADS_PER_GROUP):
            sl = slice(hd * HEAD_DIM, (hd + 1) * HEAD_DIM)
            q = q_ref[pl.ds(pl.multiple_of(c * n + r0, DIL_SUB), DIL_SUB), sl]
            k = kw_ref[c, pl.ds(r0, win), sl]
            v = vw_ref[c, pl.ds(r0, win), sl]
            s = lax.dot_general(q, k, (((1,), (1,)), ((), ())), preferred_element_type=F32) + bias_ref[hd]
            s = jnp.where(key_ok, s, MASK_VALUE)
            m = jnp.max(s, axis=-1, keepdims=True)
            p = jnp.exp(s - m)
            l = jnp.sum(p, axis=-1, keepdims=True)
            o = jnp.dot(p.astype(BF16), v, preferred_element_type=F32) / l
            if dil == 1:
                o_ref[token_rows, sl] = o.astype(BF16)
            else:
                stage_ref[hd, token_rows, :] = o
            lse_cols.append(jnp.broadcast_to(m + jnp.log(l), (DIL_SUB, lanes_per_head)))
        lse_ref[token_rows, :] = jnp.concatenate(lse_cols, axis=-1)

    for c in range(dil):
        rows = slice(c * n, (c + 1) * n)
        kw_ref[c, 0:N_SIDE, :] = kp_ref[c]
        kw_ref[c, N_SIDE:N_SIDE + n, :] = k_ref[rows, :]
        kw_ref[c, N_SIDE + n:, :] = kn_ref[c]
        vw_ref[c, 0:N_SIDE, :] = vp_ref[c]
        vw_ref[c, N_SIDE:N_SIDE + n, :] = v_ref[rows, :]
        vw_ref[c, N_SIDE + n:, :] = vn_ref[c]

    subs_per_class = n // DIL_SUB

    def one_sub_block(sb, carry):
        sub_block(sb // subs_per_class, sb % subs_per_class)
        return carry

    lax.fori_loop(0, SUPER_TILE // DIL_SUB, one_sub_block, 0, unroll=DIL_UNROLL)

    if dil > 1:
        for hd in range(A_HEADS_PER_GROUP):
            o_ref[:, hd * HEAD_DIM:(hd + 1) * HEAD_DIM] = stage_ref[hd].astype(BF16)


def _dilated_group(h3, bias, group, dil):
    batch, seq, _ = h3.shape
    tiles = seq // SUPER_TILE
    n = SUPER_TILE // dil
    halos = n // N_SIDE
    h6 = h3.reshape(batch, tiles, dil, halos, N_SIDE, IN_WIDTH)

    def cur(tile):
        return pl.BlockSpec((None, SUPER_TILE, COL_TILE), lambda b, t: (b, t, tile + group))

    def prev(tile):
        return pl.BlockSpec((None, None, dil, None, N_SIDE, COL_TILE),
                            lambda b, t: (b, jnp.maximum(t - 1, 0), 0, halos - 1, 0, tile + group))

    def nxt(tile):
        return pl.BlockSpec((None, None, dil, None, N_SIDE, COL_TILE),
                            lambda b, t: (b, jnp.minimum(t + 1, tiles - 1), 0, 0, 0, tile + group))

    kernel = functools.partial(_dilated_kernel, dil=dil, length=seq // dil)
    o, lse = pl.pallas_call(
        kernel,
        grid=(batch, tiles),
        in_specs=[
            cur(T_AQ), cur(T_AK), cur(T_AV), prev(T_AK), nxt(T_AK), prev(T_AV), nxt(T_AV),
            pl.BlockSpec((A_HEADS_PER_GROUP, DIL_SUB, DIL_SUB + 2 * N_SIDE), lambda b, t: (0, 0, 0)),
        ],
        out_specs=[
            pl.BlockSpec((None, SUPER_TILE, A_WIDTH), lambda b, t: (b, t, 0)),
            pl.BlockSpec((None, SUPER_TILE, HEAD_DIM), lambda b, t: (b, t, 0)),
        ],
        out_shape=[
            jax.ShapeDtypeStruct((batch, seq, A_WIDTH), BF16),
            jax.ShapeDtypeStruct((batch, seq, HEAD_DIM), F32),
        ],
        scratch_shapes=[
            pltpu.VMEM((dil, n + 2 * N_SIDE, COL_TILE), BF16),
            pltpu.VMEM((dil, n + 2 * N_SIDE, COL_TILE), BF16),
            pltpu.VMEM((A_HEADS_PER_GROUP, SUPER_TILE, HEAD_DIM), F32),
        ],
        compiler_params=pltpu.CompilerParams(
            dimension_semantics=("arbitrary", "arbitrary"), vmem_limit_bytes=VMEM_LIMIT_BYTES),
        name=f"dilated_g{group}",
    )(h3, h3, h3, h6, h6, h6, h6, bias)
    return o.reshape(batch * seq, A_WIDTH), lse.reshape(batch * seq, HEAD_DIM)


def _post_kernel(x_ref, o0_ref, o1_ref, o2_ref, l0_ref, l1_ref, l2_ref, ag_ref, yb_ref,
                 ga0_ref, ga1_ref, gb0_ref, gb1_ref, wpa_ref, wpb_ref, wo_ref, lng_ref, lnb_ref, y_ref):
    lanes_per_head = HEAD_DIM // A_HEADS_PER_GROUP
    for rc in range(x_ref.shape[0] // POST_CHUNK):
        rows = slice(rc * POST_CHUNK, (rc + 1) * POST_CHUNK)
        l0, l1, l2 = l0_ref[rows, :], l1_ref[rows, :], l2_ref[rows, :]
        top = jnp.maximum(jnp.maximum(l0, l1), l2)
        e0, e1, e2 = jnp.exp(l0 - top), jnp.exp(l1 - top), jnp.exp(l2 - top)
        inv = 1.0 / (e0 + e1 + e2)
        ya_heads = []
        for hd in range(A_HEADS_PER_GROUP):
            sl = slice(hd * HEAD_DIM, (hd + 1) * HEAD_DIM)
            pick = slice(hd * lanes_per_head, hd * lanes_per_head + 1)
            w0 = jnp.broadcast_to((e0 * inv)[:, pick], (POST_CHUNK, HEAD_DIM))
            w1 = jnp.broadcast_to((e1 * inv)[:, pick], (POST_CHUNK, HEAD_DIM))
            w2 = jnp.broadcast_to((e2 * inv)[:, pick], (POST_CHUNK, HEAD_DIM))
            mix = (w0 * o0_ref[rows, sl].astype(F32) + w1 * o1_ref[rows, sl].astype(F32)
                   + w2 * o2_ref[rows, sl].astype(F32))
            ya_heads.append((mix * ag_ref[rows, sl].astype(F32)).astype(BF16))
        ya = jnp.concatenate(ya_heads, axis=-1)
        pa = jnp.dot(ya, wpa_ref[...], preferred_element_type=F32)
        pb = jnp.dot(yb_ref[rows, :], wpb_ref[...], preferred_element_type=F32)
        gate_a = jnp.concatenate([ga0_ref[rows, :], ga1_ref[rows, :]], axis=-1).astype(F32)
        gate_b = jnp.concatenate([gb0_ref[rows, :], gb1_ref[rows, :]], axis=-1).astype(F32)
        merged = (gate_a * pa + gate_b * pb).astype(BF16)
        z = ALPHA * x_ref[rows, :] + jnp.dot(merged, wo_ref[...], preferred_element_type=F32)
        mu = jnp.mean(z, axis=-1, keepdims=True)
        zc = z - mu
        var = jnp.mean(zc * zc, axis=-1, keepdims=True)
        y_ref[rows, :] = zc * lax.rsqrt(var + LN_EPS) * lng_ref[...] + lnb_ref[...]


def _post(x, outs, lses, h, yb, wpa, wpb, wo, ln_g, ln_b, layer):
    tokens = x.shape[0]
    tm = POST_ROWS
    row = lambda width, col=0: pl.BlockSpec((tm, width), lambda i, col=col: (i, col))
    full = lambda a: pl.BlockSpec((None,) + a.shape[1:], lambda i: (layer, 0, 0), pipeline_mode=pl.Buffered(1))
    return pl.pallas_call(
        _post_kernel,
        grid=(tokens // tm,),
        in_specs=[
            row(D_MODEL),
            row(A_WIDTH), row(A_WIDTH), row(A_WIDTH),
            row(HEAD_DIM), row(HEAD_DIM), row(HEAD_DIM),
            row(COL_TILE, T_AG),
            row(B_HEADS * HEAD_DIM),
            row(COL_TILE, T_GATE_A), row(COL_TILE, T_GATE_A + 1),
            row(COL_TILE, T_GATE_B), row(COL_TILE, T_GATE_B + 1),
            full(wpa), full(wpb), full(wo), full(ln_g), full(ln_b),
        ],
        out_specs=row(D_MODEL),
        out_shape=jax.ShapeDtypeStruct((tokens, D_MODEL), F32),
        compiler_params=pltpu.CompilerParams(
            dimension_semantics=("arbitrary",), vmem_limit_bytes=VMEM_LIMIT_BYTES),
        name="post",
    )(x, *outs, *lses, h, yb, h, h, h, h, wpa, wpb, wo, ln_g, ln_b)


def _rope_tables(seq):
    rows = seq // GRID_W
    axis_dim = HEAD_DIM // 2
    inv_freq = ROPE_THETA ** (-jnp.arange(0, axis_dim, 2, dtype=F32) / axis_dim)
    ang_r = jnp.arange(rows).astype(F32)[:, None] * inv_freq[None]
    ang_c = jnp.arange(GRID_W).astype(F32)[:, None] * inv_freq[None]

    def over_positions(per_row, per_col):
        r = jnp.broadcast_to(per_row[:, None, :], (rows, GRID_W, axis_dim // 2))
        c = jnp.broadcast_to(per_col[None, :, :], (rows, GRID_W, axis_dim // 2))
        return r, c

    cos_r, cos_c = over_positions(jnp.cos(ang_r), jnp.cos(ang_c))
    sin_r, sin_c = over_positions(jnp.sin(ang_r), jnp.sin(ang_c))
    cos = jnp.concatenate([cos_r, cos_c, cos_r, cos_c], axis=-1).reshape(seq, HEAD_DIM)
    sin = jnp.concatenate([-sin_r, -sin_c, sin_r, sin_c], axis=-1).reshape(seq, HEAD_DIM)
    return cos, sin


def _pair_rotary_halves(a, first_col, n_heads):
    seg = a[..., first_col:first_col + n_heads * HEAD_DIM]
    seg = seg.reshape(*seg.shape[:-1], n_heads, 2, 2, HEAD_DIM // 4)
    seg = jnp.swapaxes(seg, -3, -2).reshape(*a.shape[:-1], n_heads * HEAD_DIM)
    return jnp.concatenate([a[..., :first_col], seg, a[..., first_col + n_heads * HEAD_DIM:]], axis=-1)


def _alibi_bias(group, dil):
    slopes = 2.0 ** (-ALIBI_MAX_BIAS * np.arange(1, A_HEADS + 1) / A_HEADS)
    slopes = slopes[group * A_HEADS_PER_GROUP:(group + 1) * A_HEADS_PER_GROUP]
    rel = np.arange(DIL_SUB + 2 * N_SIDE)[None, :] - N_SIDE - np.arange(DIL_SUB)[:, None]
    inside = np.abs(rel) <= N_SIDE
    bias = -slopes[:, None, None] * (dil * np.abs(rel)).astype(np.float32)[None]
    return jnp.asarray(np.where(inside[None], bias, MASK_VALUE), dtype=F32)


def _trunk(x, params, biases, cos, sin):
    batch, seq, _ = x.shape
    xf = x.reshape(batch * seq, D_MODEL)
    w_in, b_in, w_rot, b_rot, q_gain, k_gain, wpa, wpb, wo, ln_g, ln_b = params
    for layer in range(DEPTH):
        h = _in_proj(xf, w_in, b_in, w_rot, b_rot, q_gain, k_gain, cos, sin, seq, layer)
        h3 = h.reshape(batch, seq, IN_WIDTH)
        yb = _gqa_flash(h3).reshape(batch * seq, B_HEADS * HEAD_DIM)
        outs, lses = [], []
        for group, (_, dil) in enumerate(A_GROUPS):
            o, lse = _dilated_group(h3, biases[group], group, dil)
            outs.append(o)
            lses.append(lse)
        xf = _post(xf, outs, lses, h, yb, wpa, wpb, wo, ln_g, ln_b, layer)
    return xf.reshape(batch, seq, D_MODEL)


def _prepare_params(w_in, b_in, q_gain, k_gain, w_proj_a, w_proj_b, w_out, ln_g, ln_b):
    rotary_heads = B_HEADS + B_KV_HEADS
    rotary_cols = slice(T_BQ * COL_TILE, (T_BQ + ROTARY_TILES) * COL_TILE)

    def tile_major(w, tiles):
        return w.astype(BF16).reshape(DEPTH, D_MODEL, tiles, COL_TILE).transpose(0, 2, 1, 3)

    return (
        tile_major(w_in, N_COL_TILES),
        b_in.reshape(DEPTH, 1, IN_WIDTH),
        tile_major(_pair_rotary_halves(w_in[..., rotary_cols], 0, rotary_heads), ROTARY_TILES),
        _pair_rotary_halves(b_in[..., rotary_cols], 0, rotary_heads).reshape(DEPTH, 1, ROTARY_TILES * COL_TILE),
        _pair_rotary_halves(q_gain, 0, 1).reshape(DEPTH, 1, HEAD_DIM),
        _pair_rotary_halves(k_gain, 0, 1).reshape(DEPTH, 1, HEAD_DIM),
        w_proj_a.astype(BF16),
        w_proj_b.astype(BF16),
        w_out.astype(BF16),
        ln_g.reshape(DEPTH, 1, D_MODEL),
        ln_b.reshape(DEPTH, 1, D_MODEL),
    )


def kernel(x_prompt, x_sample, w_in, b_in, q_gain, k_gain, w_proj_a, w_proj_b, w_out, ln_g, ln_b):
    params = _prepare_params(w_in, b_in, q_gain, k_gain, w_proj_a, w_proj_b, w_out, ln_g, ln_b)
    biases = [_alibi_bias(group, dil) for group, (_, dil) in enumerate(A_GROUPS)]
    cos, sin = _rope_tables(max(x_prompt.shape[1], x_sample.shape[1]))
    return (_trunk(x_prompt, params, biases, cos, sin), _trunk(x_sample, params, biases, cos, sin))
```

```python
import functools
import math

import jax
import jax.numpy as jnp
import numpy as np
from jax import lax
from jax.experimental import pallas as pl
from jax.experimental.pallas import tpu as pltpu

F32 = jnp.float32
BF16 = jnp.bfloat16

D_MODEL = 1024
DEPTH = 2
HEAD_DIM = 128
GRID_W = 64
A_GROUPS = ((128, 1), (512, 4), (2048, 16))
A_HEADS_PER_GROUP = 4
A_HEADS = 12
A_WIDTH = 512
ALIBI_MAX_BIAS = 8.0
B_HEADS = 8
B_KV_HEADS = 2
B_GROUP = B_HEADS // B_KV_HEADS
ROPE_THETA = 10000.0
IN_WIDTH = 9728
RMS_EPS = 1e-6
LN_EPS = 1e-5
MASK_VALUE = -1e30
ALPHA = (2 * DEPTH) ** 0.25
QK_SCALE = 1.0 / math.sqrt(HEAD_DIM)
BQ_SCALE = QK_SCALE * math.log2(math.e)
SUBLANES = 8
LOGIT_BOUND = 60.0

COL_TILE = 512
N_COL_TILES = IN_WIDTH // COL_TILE
T_AQ, T_AK, T_AV, T_AG = 0, 3, 6, 9
T_BQ, T_BKV, T_BG = 10, 12, 13
T_GATE_A, T_GATE_B = 15, 17
ROTARY_TILES = T_BKV + 1 - T_BQ
TILES_PER_STEP = 2
IN_PROJ_STEPS = -(-N_COL_TILES // TILES_PER_STEP)
SLAB_WIDTH = IN_PROJ_STEPS * TILES_PER_STEP * COL_TILE
BK_COL128 = T_BKV * COL_TILE // HEAD_DIM
BV_COL128 = BK_COL128 + B_KV_HEADS

N_SIDE = 64

VMEM_LIMIT_BYTES = 56 * 1024 * 1024

SUPER_TILE = 2048
IN_PROJ_CHUNK = 512
GATHER_STRIDE = 4
FLASH_Q_ROWS_CHOICES = (1024, 512)
FLASH_SCORE_ELEMS = 4 * 1024 * 1024
DIL_SUB = 128
DIL_UNROLL = 8
POST_ROWS = 1024
POST_CHUNK = 512


def _sigmoid(x):
    return 0.5 * jnp.tanh(0.5 * x) + 0.5


def _in_proj_kernel(x_ref, w0_ref, w1_ref, b0_ref, b1_ref, wrot0_ref, wrot1_ref, brot0_ref, brot1_ref,
                    qg_ref, kg_ref, cos_ref, sin_ref, avg_ref, o_ref, xb_ref, hs_ref, hs2_ref):
    step = pl.program_id(1)
    tm = x_ref.shape[0]

    @pl.when(step == 0)
    def _():
        xb_ref[...] = x_ref[...].astype(BF16)

    lane_tiles = COL_TILE // HEAD_DIM
    n_chunks = tm // IN_PROJ_CHUNK

    def column_tile(slot, epilogue, weights, bias):
        cols = slot * COL_TILE
        for rc in range(n_chunks):
            rows = slice(rc * IN_PROJ_CHUNK, (rc + 1) * IN_PROJ_CHUNK)
            h = jnp.dot(xb_ref[rows, :], weights[...], preferred_element_type=F32) + bias[...]
            epilogue(slot, cols, rc, rows, h)

    def head_mean_squares(h):
        width = h.shape[1]
        return jnp.dot((h * h).astype(BF16), avg_ref[0:width, 0:width], preferred_element_type=F32)

    def rms_rope(hh, ms, gain, rows):
        y = hh * lax.rsqrt(ms + RMS_EPS) * gain
        return y * cos_ref[rows, :] + pltpu.roll(y, HEAD_DIM // 2, 1) * sin_ref[rows, :]

    def plain(a_scale):
        def epilogue(slot, cols, rc, rows, h):
            o_ref[rows, cols:cols + COL_TILE] = (h * a_scale).astype(BF16)
        return epilogue

    def class_major(dil, a_scale):
        n = tm // dil
        per_chunk = IN_PROJ_CHUNK // dil

        def epilogue(slot, cols, rc, rows, h):
            hs = hs_ref.at[slot]
            for ln in range(lane_tiles):
                hs[rc, ln] = h[:, ln * HEAD_DIM:(ln + 1) * HEAD_DIM] * a_scale
            src_ref, stride = hs, dil
            if dil == GATHER_STRIDE * GATHER_STRIDE:
                quarter = IN_PROJ_CHUNK // GATHER_STRIDE
                for ln in range(lane_tiles):
                    for c4 in range(GATHER_STRIDE):
                        hs2_ref[rc, ln, c4 * quarter:(c4 + 1) * quarter, :] = (
                            hs[rc, ln, pl.ds(c4, quarter, stride=GATHER_STRIDE), :])
                src_ref, stride = hs2_ref, GATHER_STRIDE
            for c in range(dil):
                if stride == dil:
                    first = c
                else:
                    first = (c % GATHER_STRIDE) * (IN_PROJ_CHUNK // GATHER_STRIDE) + c // GATHER_STRIDE
                dst = slice(c * n + rc * per_chunk, c * n + (rc + 1) * per_chunk)
                for ln in range(lane_tiles):
                    o_ref[dst, cols + ln * HEAD_DIM:cols + (ln + 1) * HEAD_DIM] = (
                        src_ref[rc, ln, pl.ds(first, per_chunk, stride=stride), :].astype(BF16))
        return epilogue

    def silu(slot, cols, rc, rows, h):
        o_ref[rows, cols:cols + COL_TILE] = (h * _sigmoid(h)).astype(BF16)

    def rope_q(slot, cols, rc, rows, h):
        ms = head_mean_squares(h)
        for hd in range(lane_tiles):
            sl = slice(hd * HEAD_DIM, (hd + 1) * HEAD_DIM)
            o_ref[rows, cols + hd * HEAD_DIM:cols + (hd + 1) * HEAD_DIM] = (
                rms_rope(h[:, sl], ms[:, sl], qg_ref[...], rows) * BQ_SCALE).astype(BF16)

    def rope_k_and_v(slot, cols, rc, rows, h):
        k_width = B_KV_HEADS * HEAD_DIM
        ms = head_mean_squares(h[:, :k_width])
        for hd in range(B_KV_HEADS):
            sl = slice(hd * HEAD_DIM, (hd + 1) * HEAD_DIM)
            o_ref[rows, cols + hd * HEAD_DIM:cols + (hd + 1) * HEAD_DIM] = (
                rms_rope(h[:, sl], ms[:, sl], kg_ref[...], rows).astype(BF16))
        o_ref[rows, cols + k_width:cols + COL_TILE] = h[:, k_width:].astype(BF16)

    def gate(slot, cols, rc, rows, h):
        o_ref[rows, cols:cols + COL_TILE] = _sigmoid(h).astype(BF16)

    plain_refs = ((w0_ref, b0_ref), (w1_ref, b1_ref))
    rotary_refs = ((wrot0_ref, brot0_ref), (wrot1_ref, brot1_ref))

    def tile_program(j):
        if j < T_AG:
            dil = A_GROUPS[j % len(A_GROUPS)][1]
            a_scale = QK_SCALE if j < T_AK else 1.0
            return (plain(a_scale) if dil == 1 else class_major(dil, a_scale)), False
        if j in (T_AG, T_BG, T_BG + 1):
            return silu, False
        if j in (T_BQ, T_BQ + 1):
            return rope_q, True
        if j == T_BKV:
            return rope_k_and_v, True
        return gate, False

    def grid_step(first_tile):
        def run():
            for slot in range(TILES_PER_STEP):
                j = first_tile + slot
                if j < N_COL_TILES:
                    epilogue, rotary = tile_program(j)
                    weights, bias = (rotary_refs if rotary else plain_refs)[slot]
                    column_tile(slot, epilogue, weights, bias)
                else:
                    o_ref[:, slot * COL_TILE:(slot + 1) * COL_TILE] = jnp.zeros((tm, COL_TILE), BF16)
        return run

    for s in range(IN_PROJ_STEPS):
        pl.when(step == s)(grid_step(s * TILES_PER_STEP))


def _in_proj(x, w, b, w_rot, b_rot, q_gain, k_gain, cos, sin, seq, layer):
    tokens = x.shape[0]
    tm = SUPER_TILE
    assert seq % tm == 0
    pos_blocks = seq // tm
    row_spec = lambda width: pl.BlockSpec((tm, width), lambda i, j: (i, 0))
    pos_spec = pl.BlockSpec((tm, HEAD_DIM), lambda i, j: (i % pos_blocks, 0))
    vec_spec = pl.BlockSpec((None, 1, HEAD_DIM), lambda i, j: (layer, 0, 0))
    head_of_col = np.arange(COL_TILE) // HEAD_DIM
    head_average = jnp.asarray((head_of_col[:, None] == head_of_col[None, :]) / HEAD_DIM, dtype=BF16)

    def tile_of(step, slot):
        return jnp.minimum(step * TILES_PER_STEP + slot, N_COL_TILES - 1)

    def rotary_tile_of(step, slot):
        return jnp.clip(step * TILES_PER_STEP + slot - T_BQ, 0, ROTARY_TILES - 1)

    def weight_specs(tile_index):
        return [pl.BlockSpec((None, None, D_MODEL, COL_TILE),
                             lambda i, s, slot=slot: (layer, tile_index(s, slot), 0, 0))
                for slot in range(TILES_PER_STEP)]

    def bias_specs(tile_index):
        return [pl.BlockSpec((None, 1, COL_TILE), lambda i, s, slot=slot: (layer, 0, tile_index(s, slot)))
                for slot in range(TILES_PER_STEP)]

    staging = (tm // IN_PROJ_CHUNK, COL_TILE // HEAD_DIM, IN_PROJ_CHUNK, HEAD_DIM)
    return pl.pallas_call(
        _in_proj_kernel,
        grid=(tokens // tm, IN_PROJ_STEPS),
        in_specs=[
            row_spec(D_MODEL),
            *weight_specs(tile_of), *bias_specs(tile_of),
            *weight_specs(rotary_tile_of), *bias_specs(rotary_tile_of),
            vec_spec, vec_spec, pos_spec, pos_spec,
            pl.BlockSpec((COL_TILE, COL_TILE), lambda i, s: (0, 0)),
        ],
        out_specs=pl.BlockSpec((tm, TILES_PER_STEP * COL_TILE), lambda i, s: (i, s)),
        out_shape=jax.ShapeDtypeStruct((tokens, SLAB_WIDTH), BF16),
        scratch_shapes=[
            pltpu.VMEM((tm, D_MODEL), BF16),
            pltpu.VMEM((TILES_PER_STEP,) + staging, F32),
            pltpu.VMEM(staging, F32),
        ],
        compiler_params=pltpu.CompilerParams(
            dimension_semantics=("arbitrary", "arbitrary"), vmem_limit_bytes=VMEM_LIMIT_BYTES),
        name="in_proj",
    )(x, *[w] * TILES_PER_STEP, *[b] * TILES_PER_STEP, *[w_rot] * TILES_PER_STEP, *[b_rot] * TILES_PER_STEP,
      q_gain, k_gain, cos, sin, head_average)


def _flash_kernel(q_ref, k_ref, v_ref, g_ref, o_ref, qt_ref, vt_ref, m_ref, l_ref, acc_ref, k2max_ref,
                  *, tq, tk, n_chunks):
    i = pl.program_id(2)

    @pl.when(i == 0)
    def _():
        def fill(c, k2max):
            start = pl.multiple_of(c * tk, tk)
            vt_ref[c] = v_ref[pl.ds(start, tk), :].astype(F32).T.astype(BF16)
            kf = k_ref[pl.ds(start, tk), :].astype(F32)
            k2 = jnp.max(jnp.sum(kf * kf, axis=-1, keepdims=True), axis=0, keepdims=True)
            return jnp.maximum(k2max, k2)

        k2max_ref[...] = lax.fori_loop(0, n_chunks, fill, jnp.zeros((1, 1), F32))

    q2 = jnp.zeros((1, 1), F32)
    for hd in range(B_GROUP):
        qf = q_ref[:, hd * HEAD_DIM:(hd + 1) * HEAD_DIM].astype(F32).T
        qt_ref[:, hd * tq:(hd + 1) * tq] = qf.astype(BF16)
        q2 = jnp.maximum(q2, jnp.max(jnp.sum(qf * qf, axis=0, keepdims=True), axis=1, keepdims=True))
    acc_ref[...] = jnp.zeros(acc_ref.shape, F32)
    l_ref[...] = jnp.zeros(l_ref.shape, F32)

    logits_bounded = (q2 * k2max_ref[...])[0, 0] <= LOGIT_BOUND * LOGIT_BOUND

    def scores(c):
        start = pl.multiple_of(c * tk, tk)
        return jnp.dot(k_ref[pl.ds(start, tk), :], qt_ref[...], preferred_element_type=F32)

    def sublane_partial_sums(p):
        return jnp.sum(p.reshape(tk // SUBLANES, SUBLANES, p.shape[1]), axis=0)

    def bounded_body(c, carry):
        p = jnp.exp2(scores(c))
        l_ref[...] += sublane_partial_sums(p)
        acc_ref[...] += jnp.dot(vt_ref[c], p.astype(BF16), preferred_element_type=F32)
        return carry

    def online_body(c, carry):
        st = scores(c)
        m_prev = m_ref[...]
        m_new = jnp.maximum(m_prev, jnp.max(st, axis=0, keepdims=True))
        alpha = jnp.exp2(m_prev - m_new)
        p = jnp.exp2(st - m_new)
        l_ref[...] = alpha * l_ref[...] + sublane_partial_sums(p)
        acc_ref[...] = alpha * acc_ref[...] + jnp.dot(vt_ref[c], p.astype(BF16), preferred_element_type=F32)
        m_ref[...] = m_new
        return carry

    @pl.when(logits_bounded)
    def _():
        lax.fori_loop(0, n_chunks, bounded_body, 0, unroll=2)

    @pl.when(jnp.logical_not(logits_bounded))
    def _():
        m_ref[...] = jnp.full(m_ref.shape, MASK_VALUE, F32)
        lax.fori_loop(0, n_chunks, online_body, 0, unroll=2)

    ot = acc_ref[...] / jnp.sum(l_ref[...], axis=0, keepdims=True)
    for hd in range(B_GROUP):
        sl = slice(hd * HEAD_DIM, (hd + 1) * HEAD_DIM)
        o_ref[:, sl] = (ot[:, hd * tq:(hd + 1) * tq].T * g_ref[:, sl].astype(F32)).astype(BF16)


def _flash_tiles(seq):
    for tq in FLASH_Q_ROWS_CHOICES:
        tk = FLASH_SCORE_ELEMS // (B_GROUP * tq)
        kv_bytes = 2 * 2 * seq * HEAD_DIM * 2 + seq * HEAD_DIM * 2
        block_bytes = 3 * 2 * tq * COL_TILE * 2
        state_bytes = B_GROUP * tq * (HEAD_DIM * (2 + 4) + SUBLANES * 4 + 4)
        score_bytes = FLASH_SCORE_ELEMS * (4 + 2)
        if seq % tq == 0 and kv_bytes + block_bytes + state_bytes + score_bytes <= VMEM_LIMIT_BYTES:
            return tq, tk
    raise ValueError(f"no flash tiling fits VMEM for sequence length {seq}")


def _gqa_flash(h3):
    batch, seq, _ = h3.shape
    tq, tk = _flash_tiles(seq)
    n_chunks = seq // tk
    kernel = functools.partial(_flash_kernel, tq=tq, tk=tk, n_chunks=n_chunks)
    return pl.pallas_call(
        kernel,
        grid=(batch, B_KV_HEADS, seq // tq),
        in_specs=[
            pl.BlockSpec((None, tq, COL_TILE), lambda b, g, i: (b, i, T_BQ + g)),
            pl.BlockSpec((None, seq, HEAD_DIM), lambda b, g, i: (b, 0, BK_COL128 + g)),
            pl.BlockSpec((None, seq, HEAD_DIM), lambda b, g, i: (b, 0, BV_COL128 + g)),
            pl.BlockSpec((None, tq, COL_TILE), lambda b, g, i: (b, i, T_BG + g)),
        ],
        out_specs=pl.BlockSpec((None, tq, COL_TILE), lambda b, g, i: (b, i, g)),
        out_shape=jax.ShapeDtypeStruct((batch, seq, B_HEADS * HEAD_DIM), BF16),
        scratch_shapes=[
            pltpu.VMEM((HEAD_DIM, B_GROUP * tq), BF16),
            pltpu.VMEM((n_chunks, HEAD_DIM, tk), BF16),
            pltpu.VMEM((1, B_GROUP * tq), F32),
            pltpu.VMEM((SUBLANES, B_GROUP * tq), F32),
            pltpu.VMEM((HEAD_DIM, B_GROUP * tq), F32),
            pltpu.VMEM((1, 1), F32),
        ],
        compiler_params=pltpu.CompilerParams(
            dimension_semantics=("arbitrary", "arbitrary", "arbitrary"), vmem_limit_bytes=VMEM_LIMIT_BYTES),
        name="gqa_flash",
    )(h3, h3, h3, h3)


def _dilated_kernel(q_ref, k_ref, v_ref, kp_ref, kn_ref, vp_ref, vn_ref, bias_ref, o_ref, lse_ref,
                    kw_ref, vw_ref, stage_ref, *, dil, length):
    n = SUPER_TILE // dil
    win = DIL_SUB + 2 * N_SIDE
    first_pos = pl.program_id(1) * n
    lanes_per_head = HEAD_DIM // A_HEADS_PER_GROUP

    def sub_block(c, a):
        r0 = pl.multiple_of(a * DIL_SUB, DIL_SUB)
        key_pos = first_pos + a * DIL_SUB - N_SIDE + lax.broadcasted_iota(jnp.int32, (1, win), 1)
        key_ok = (key_pos >= 0) & (key_pos < length)
        token_rows = pl.ds(r0, DIL_SUB) if dil == 1 else pl.ds(c + dil * r0, DIL_SUB, stride=dil)
        lse_cols = []
        for hd in range(A_HEADS_PER_GROUP):
            sl = slice(hd * HEAD_DIM, (hd + 1) * HEAD_DIM)
            q = q_ref[pl.ds(pl.multiple_of(c * n + r0, DIL_SUB), DIL_SUB), sl]
            k = kw_ref[c, pl.ds(r0, win), sl]
            v = vw_ref[c, pl.ds(r0, win), sl]
            s = lax.dot_general(q, k, (((1,), (1,)), ((), ())), preferred_element_type=F32) + bias_ref[hd]
            s = jnp.where(key_ok, s, MASK_VALUE)
            m = jnp.max(s, axis=-1, keepdims=True)
            p = jnp.exp(s - m)
            l = jnp.sum(p, axis=-1, keepdims=True)
            o = jnp.dot(p.astype(BF16), v, preferred_element_type=F32) / l
            if dil == 1:
                o_ref[token_rows, sl] = o.astype(BF16)
            else:
                stage_ref[hd, token_rows, :] = o
            lse_cols.append(jnp.broadcast_to(m + jnp.log(l), (DIL_SUB, lanes_per_head)))
        lse_ref[token_rows, :] = jnp.concatenate(lse_cols, axis=-1)

    for c in range(dil):
        rows = slice(c * n, (c + 1) * n)
        kw_ref[c, 0:N_SIDE, :] = kp_ref[c]
        kw_ref[c, N_SIDE:N_SIDE + n, :] = k_ref[rows, :]
        kw_ref[c, N_SIDE + n:, :] = kn_ref[c]
        vw_ref[c, 0:N_SIDE, :] = vp_ref[c]
        vw_ref[c, N_SIDE:N_SIDE + n, :] = v_ref[rows, :]
        vw_ref[c, N_SIDE + n:, :] = vn_ref[c]

    subs_per_class = n // DIL_SUB

    def one_sub_block(sb, carry):
        sub_block(sb // subs_per_class, sb % subs_per_class)
        return carry

    lax.fori_loop(0, SUPER_TILE // DIL_SUB, one_sub_block, 0, unroll=DIL_UNROLL)

    if dil > 1:
        for hd in range(A_HEADS_PER_GROUP):
            o_ref[:, hd * HEAD_DIM:(hd + 1) * HEAD_DIM] = stage_ref[hd].astype(BF16)


def _dilated_group(h3, bias, group, dil):
    batch, seq, _ = h3.shape
    tiles = seq // SUPER_TILE
    n = SUPER_TILE // dil
    halos = n // N_SIDE
    h6 = h3.reshape(batch, tiles, dil, halos, N_SIDE, SLAB_WIDTH)

    def cur(tile):
        return pl.BlockSpec((None, SUPER_TILE, COL_TILE), lambda b, t: (b, t, tile + group))

    def prev(tile):
        return pl.BlockSpec((None, None, dil, None, N_SIDE, COL_TILE),
                            lambda b, t: (b, jnp.maximum(t - 1, 0), 0, halos - 1, 0, tile + group))

    def nxt(tile):
        return pl.BlockSpec((None, None, dil, None, N_SIDE, COL_TILE),
                            lambda b, t: (b, jnp.minimum(t + 1, tiles - 1), 0, 0, 0, tile + group))

    kernel = functools.partial(_dilated_kernel, dil=dil, length=seq // dil)
    o, lse = pl.pallas_call(
        kernel,
        grid=(batch, tiles),
        in_specs=[
            cur(T_AQ), cur(T_AK), cur(T_AV), prev(T_AK), nxt(T_AK), prev(T_AV), nxt(T_AV),
            pl.BlockSpec((A_HEADS_PER_GROUP, DIL_SUB, DIL_SUB + 2 * N_SIDE), lambda b, t: (0, 0, 0)),
        ],
        out_specs=[
            pl.BlockSpec((None, SUPER_TILE, A_WIDTH), lambda b, t: (b, t, 0)),
            pl.BlockSpec((None, SUPER_TILE, HEAD_DIM), lambda b, t: (b, t, 0)),
        ],
        out_shape=[
            jax.ShapeDtypeStruct((batch, seq, A_WIDTH), BF16),
            jax.ShapeDtypeStruct((batch, seq, HEAD_DIM), F32),
        ],
        scratch_shapes=[
            pltpu.VMEM((dil, n + 2 * N_SIDE, COL_TILE), BF16),
            pltpu.VMEM((dil, n + 2 * N_SIDE, COL_TILE), BF16),
            pltpu.VMEM((A_HEADS_PER_GROUP, SUPER_TILE, HEAD_DIM), F32),
        ],
        compiler_params=pltpu.CompilerParams(
            dimension_semantics=("arbitrary", "arbitrary"), vmem_limit_bytes=VMEM_LIMIT_BYTES),
        name=f"dilated_g{group}",
    )(h3, h3, h3, h6, h6, h6, h6, bias)
    return o.reshape(batch * seq, A_WIDTH), lse.reshape(batch * seq, HEAD_DIM)


def _post_kernel(x_ref, o0_ref, o1_ref, o2_ref, l0_ref, l1_ref, l2_ref, ag_ref, yb_ref,
                 ga0_ref, ga1_ref, gb0_ref, gb1_ref, wpa_ref, wpb_ref, wo_ref, lng_ref, lnb_ref, y_ref):
    lanes_per_head = HEAD_DIM // A_HEADS_PER_GROUP
    for rc in range(x_ref.shape[0] // POST_CHUNK):
        rows = slice(rc * POST_CHUNK, (rc + 1) * POST_CHUNK)
        l0, l1, l2 = l0_ref[rows, :], l1_ref[rows, :], l2_ref[rows, :]
        top = jnp.maximum(jnp.maximum(l0, l1), l2)
        e0, e1, e2 = jnp.exp(l0 - top), jnp.exp(l1 - top), jnp.exp(l2 - top)
        inv = 1.0 / (e0 + e1 + e2)
        ya_heads = []
        for hd in range(A_HEADS_PER_GROUP):
            sl = slice(hd * HEAD_DIM, (hd + 1) * HEAD_DIM)
            pick = slice(hd * lanes_per_head, hd * lanes_per_head + 1)
            w0 = jnp.broadcast_to((e0 * inv)[:, pick], (POST_CHUNK, HEAD_DIM))
            w1 = jnp.broadcast_to((e1 * inv)[:, pick], (POST_CHUNK, HEAD_DIM))
            w2 = jnp.broadcast_to((e2 * inv)[:, pick], (POST_CHUNK, HEAD_DIM))
            mix = (w0 * o0_ref[rows, sl].astype(F32) + w1 * o1_ref[rows, sl].astype(F32)
                   + w2 * o2_ref[rows, sl].astype(F32))
            ya_heads.append((mix * ag_ref[rows, sl].astype(F32)).astype(BF16))
        ya = jnp.concatenate(ya_heads, axis=-1)
        pa = jnp.dot(ya, wpa_ref[...], preferred_element_type=F32)
        pb = jnp.dot(yb_ref[rows, :], wpb_ref[...], preferred_element_type=F32)
        gate_a = jnp.concatenate([ga0_ref[rows, :], ga1_ref[rows, :]], axis=-1).astype(F32)
        gate_b = jnp.concatenate([gb0_ref[rows, :], gb1_ref[rows, :]], axis=-1).astype(F32)
        merged = (gate_a * pa + gate_b * pb).astype(BF16)
        z = ALPHA * x_ref[rows, :] + jnp.dot(merged, wo_ref[...], preferred_element_type=F32)
        mu = jnp.mean(z, axis=-1, keepdims=True)
        zc = z - mu
        var = jnp.mean(zc * zc, axis=-1, keepdims=True)
        y_ref[rows, :] = zc * lax.rsqrt(var + LN_EPS) * lng_ref[...] + lnb_ref[...]


def _post(x, outs, lses, h, yb, wpa, wpb, wo, ln_g, ln_b, layer):
    tokens = x.shape[0]
    tm = POST_ROWS
    row = lambda width, col=0: pl.BlockSpec((tm, width), lambda i, col=col: (i, col))
    full = lambda a: pl.BlockSpec((None,) + a.shape[1:], lambda i: (layer, 0, 0), pipeline_mode=pl.Buffered(1))
    return pl.pallas_call(
        _post_kernel,
        grid=(tokens // tm,),
        in_specs=[
            row(D_MODEL),
            row(A_WIDTH), row(A_WIDTH), row(A_WIDTH),
            row(HEAD_DIM), row(HEAD_DIM), row(HEAD_DIM),
            row(COL_TILE, T_AG),
            row(B_HEADS * HEAD_DIM),
            row(COL_TILE, T_GATE_A), row(COL_TILE, T_GATE_A + 1),
            row(COL_TILE, T_GATE_B), row(COL_TILE, T_GATE_B + 1),
            full(wpa), full(wpb), full(wo), full(ln_g), full(ln_b),
        ],
        out_specs=row(D_MODEL),
        out_shape=jax.ShapeDtypeStruct((tokens, D_MODEL), F32),
        compiler_params=pltpu.CompilerParams(
            dimension_semantics=("arbitrary",), vmem_limit_bytes=VMEM_LIMIT_BYTES),
        name="post",
    )(x, *outs, *lses, h, yb, h, h, h, h, wpa, wpb, wo, ln_g, ln_b)


def _rope_tables(seq):
    rows = seq // GRID_W
    axis_dim = HEAD_DIM // 2
    inv_freq = ROPE_THETA ** (-jnp.arange(0, axis_dim, 2, dtype=F32) / axis_dim)
    ang_r = jnp.arange(rows).astype(F32)[:, None] * inv_freq[None]
    ang_c = jnp.arange(GRID_W).astype(F32)[:, None] * inv_freq[None]

    def over_positions(per_row, per_col):
        r = jnp.broadcast_to(per_row[:, None, :], (rows, GRID_W, axis_dim // 2))
        c = jnp.broadcast_to(per_col[None, :, :], (rows, GRID_W, axis_dim // 2))
        return r, c

    cos_r, cos_c = over_positions(jnp.cos(ang_r), jnp.cos(ang_c))
    sin_r, sin_c = over_positions(jnp.sin(ang_r), jnp.sin(ang_c))
    cos = jnp.concatenate([cos_r, cos_c, cos_r, cos_c], axis=-1).reshape(seq, HEAD_DIM)
    sin = jnp.concatenate([-sin_r, -sin_c, sin_r, sin_c], axis=-1).reshape(seq, HEAD_DIM)
    return cos, sin


def _pair_rotary_halves(a, first_col, n_heads):
    seg = a[..., first_col:first_col + n_heads * HEAD_DIM]
    seg = seg.reshape(*seg.shape[:-1], n_heads, 2, 2, HEAD_DIM // 4)
    seg = jnp.swapaxes(seg, -3, -2).reshape(*a.shape[:-1], n_heads * HEAD_DIM)
    return jnp.concatenate([a[..., :first_col], seg, a[..., first_col + n_heads * HEAD_DIM:]], axis=-1)


def _alibi_bias(group, dil):
    slopes = 2.0 ** (-ALIBI_MAX_BIAS * np.arange(1, A_HEADS + 1) / A_HEADS)
    slopes = slopes[group * A_HEADS_PER_GROUP:(group + 1) * A_HEADS_PER_GROUP]
    rel = np.arange(DIL_SUB + 2 * N_SIDE)[None, :] - N_SIDE - np.arange(DIL_SUB)[:, None]
    inside = np.abs(rel) <= N_SIDE
    bias = -slopes[:, None, None] * (dil * np.abs(rel)).astype(np.float32)[None]
    return jnp.asarray(np.where(inside[None], bias, MASK_VALUE), dtype=F32)


def _trunk(x, params, biases, cos, sin):
    batch, seq, _ = x.shape
    xf = x.reshape(batch * seq, D_MODEL)
    w_in, b_in, w_rot, b_rot, q_gain, k_gain, wpa, wpb, wo, ln_g, ln_b = params
    for layer in range(DEPTH):
        h = _in_proj(xf, w_in, b_in, w_rot, b_rot, q_gain, k_gain, cos, sin, seq, layer)
        h3 = h.reshape(batch, seq, SLAB_WIDTH)
        yb = _gqa_flash(h3).reshape(batch * seq, B_HEADS * HEAD_DIM)
        outs, lses = [], []
        for group, (_, dil) in enumerate(A_GROUPS):
            o, lse = _dilated_group(h3, biases[group], group, dil)
            outs.append(o)
            lses.append(lse)
        xf = _post(xf, outs, lses, h, yb, wpa, wpb, wo, ln_g, ln_b, layer)
    return xf.reshape(batch, seq, D_MODEL)


def _prepare_params(w_in, b_in, q_gain, k_gain, w_proj_a, w_proj_b, w_out, ln_g, ln_b):
    rotary_heads = B_HEADS + B_KV_HEADS
    rotary_cols = slice(T_BQ * COL_TILE, (T_BQ + ROTARY_TILES) * COL_TILE)

    def tile_major(w, tiles):
        return w.astype(BF16).reshape(DEPTH, D_MODEL, tiles, COL_TILE).transpose(0, 2, 1, 3)

    return (
        tile_major(w_in, N_COL_TILES),
        b_in.reshape(DEPTH, 1, IN_WIDTH),
        tile_major(_pair_rotary_halves(w_in[..., rotary_cols], 0, rotary_heads), ROTARY_TILES),
        _pair_rotary_halves(b_in[..., rotary_cols], 0, rotary_heads).reshape(DEPTH, 1, ROTARY_TILES * COL_TILE),
        _pair_rotary_halves(q_gain, 0, 1).reshape(DEPTH, 1, HEAD_DIM),
        _pair_rotary_halves(k_gain, 0, 1).reshape(DEPTH, 1, HEAD_DIM),
        w_proj_a.astype(BF16),
        w_proj_b.astype(BF16),
        w_out.astype(BF16),
        ln_g.reshape(DEPTH, 1, D_MODEL),
        ln_b.reshape(DEPTH, 1, D_MODEL),
    )


def kernel(x_prompt, x_sample, w_in, b_in, q_gain, k_gain, w_proj_a, w_proj_b, w_out, ln_g, ln_b):
    params = _prepare_params(w_in, b_in, q_gain, k_gain, w_proj_a, w_proj_b, w_out, ln_g, ln_b)
    biases = [_alibi_bias(group, dil) for group, (_, dil) in enumerate(A_GROUPS)]
    cos, sin = _rope_tables(max(x_prompt.shape[1], x_sample.shape[1]))
    return (_trunk(x_prompt, params, biases, cos, sin), _trunk(x_sample, params, biases, cos, sin))
```

```python
import functools
import math

import jax
import jax.numpy as jnp
import numpy as np
from jax import lax
from jax.experimental import pallas as pl
from jax.experimental.pallas import tpu as pltpu

F32 = jnp.float32
BF16 = jnp.bfloat16

D_MODEL = 1024
DEPTH = 2
HEAD_DIM = 128
GRID_W = 64
A_GROUPS = ((128, 1), (512, 4), (2048, 16))
A_HEADS_PER_GROUP = 4
A_HEADS = 12
A_WIDTH = 512
ALIBI_MAX_BIAS = 8.0
B_HEADS = 8
B_KV_HEADS = 2
B_GROUP = B_HEADS // B_KV_HEADS
ROPE_THETA = 10000.0
IN_WIDTH = 9728
RMS_EPS = 1e-6
LN_EPS = 1e-5
MASK_VALUE = -1e30
ALPHA = (2 * DEPTH) ** 0.25
QK_SCALE = 1.0 / math.sqrt(HEAD_DIM)
BQ_SCALE = QK_SCALE * math.log2(math.e)
SUBLANES = 8
LOGIT_BOUND = 60.0

COL_TILE = 512
N_COL_TILES = IN_WIDTH // COL_TILE
T_AQ, T_AK, T_AV, T_AG = 0, 3, 6, 9
T_BQ, T_BKV, T_BG = 10, 12, 13
T_GATE_A, T_GATE_B = 15, 17
ROTARY_TILES = T_BKV + 1 - T_BQ
BK_COL128 = T_BKV * COL_TILE // HEAD_DIM
BV_COL128 = BK_COL128 + B_KV_HEADS

N_SIDE = 64

VMEM_LIMIT_BYTES = 56 * 1024 * 1024

SUPER_TILE = 2048
IN_PROJ_CHUNK = 512
GATHER_STRIDE = 4
FLASH_Q_ROWS_CHOICES = (1024, 512)
FLASH_SCORE_ELEMS = 4 * 1024 * 1024
DIL_SUB = 128
DIL_UNROLL = 8
POST_ROWS = 1024
POST_CHUNK = 512


def _sigmoid(x):
    return 0.5 * jnp.tanh(0.5 * x) + 0.5


def _in_proj_kernel(x_ref, w_ref, b_ref, wrot_ref, brot_ref, qg_ref, kg_ref, cos_ref, sin_ref, avg_ref,
                    o_ref, xb_ref, hs_ref, hs2_ref):
    j = pl.program_id(1)
    tm = x_ref.shape[0]

    @pl.when(j == 0)
    def _():
        xb_ref[...] = x_ref[...].astype(BF16)

    lane_tiles = COL_TILE // HEAD_DIM
    n_chunks = tm // IN_PROJ_CHUNK

    def column_tile(epilogue, weights=w_ref, bias=b_ref):
        def run():
            for rc in range(n_chunks):
                rows = slice(rc * IN_PROJ_CHUNK, (rc + 1) * IN_PROJ_CHUNK)
                h = jnp.dot(xb_ref[rows, :], weights[...], preferred_element_type=F32) + bias[...]
                epilogue(rc, rows, h)
        return run

    def head_mean_squares(h):
        width = h.shape[1]
        return jnp.dot((h * h).astype(BF16), avg_ref[0:width, 0:width], preferred_element_type=F32)

    def rms_rope(hh, ms, gain, rows):
        y = hh * lax.rsqrt(ms + RMS_EPS) * gain
        return y * cos_ref[rows, :] + pltpu.roll(y, HEAD_DIM // 2, 1) * sin_ref[rows, :]

    a_scale = jnp.where(j < T_AK, QK_SCALE, 1.0)
    a_group = j % len(A_GROUPS)

    def plain(rc, rows, h):
        o_ref[rows, :] = (h * a_scale).astype(BF16)

    def class_major(dil):
        n = tm // dil
        per_chunk = IN_PROJ_CHUNK // dil

        def epilogue(rc, rows, h):
            for ln in range(lane_tiles):
                hs_ref[rc, ln] = h[:, ln * HEAD_DIM:(ln + 1) * HEAD_DIM] * a_scale
            src_ref, stride = hs_ref, dil
            if dil == GATHER_STRIDE * GATHER_STRIDE:
                quarter = IN_PROJ_CHUNK // GATHER_STRIDE
                for ln in range(lane_tiles):
                    for c4 in range(GATHER_STRIDE):
                        hs2_ref[rc, ln, c4 * quarter:(c4 + 1) * quarter, :] = (
                            hs_ref[rc, ln, pl.ds(c4, quarter, stride=GATHER_STRIDE), :])
                src_ref, stride = hs2_ref, GATHER_STRIDE
            for c in range(dil):
                if src_ref is hs_ref:
                    first = c
                else:
                    first = (c % GATHER_STRIDE) * (IN_PROJ_CHUNK // GATHER_STRIDE) + c // GATHER_STRIDE
                dst = slice(c * n + rc * per_chunk, c * n + (rc + 1) * per_chunk)
                for ln in range(lane_tiles):
                    o_ref[dst, ln * HEAD_DIM:(ln + 1) * HEAD_DIM] = (
                        src_ref[rc, ln, pl.ds(first, per_chunk, stride=stride), :].astype(BF16))
        return epilogue

    def silu(rc, rows, h):
        o_ref[rows, :] = (h * _sigmoid(h)).astype(BF16)

    def rope_q(rc, rows, h):
        ms = head_mean_squares(h)
        for hd in range(lane_tiles):
            sl = slice(hd * HEAD_DIM, (hd + 1) * HEAD_DIM)
            o_ref[rows, sl] = (rms_rope(h[:, sl], ms[:, sl], qg_ref[...], rows) * BQ_SCALE).astype(BF16)

    def rope_k_and_v(rc, rows, h):
        k_width = B_KV_HEADS * HEAD_DIM
        ms = head_mean_squares(h[:, :k_width])
        for hd in range(B_KV_HEADS):
            sl = slice(hd * HEAD_DIM, (hd + 1) * HEAD_DIM)
            o_ref[rows, sl] = rms_rope(h[:, sl], ms[:, sl], kg_ref[...], rows).astype(BF16)
        o_ref[rows, k_width:] = h[:, k_width:].astype(BF16)

    def gate(rc, rows, h):
        o_ref[rows, :] = _sigmoid(h).astype(BF16)

    for group, (_, dil) in enumerate(A_GROUPS):
        pl.when((j < T_AG) & (a_group == group))(column_tile(plain if dil == 1 else class_major(dil)))
    pl.when((j == T_AG) | (j == T_BG) | (j == T_BG + 1))(column_tile(silu))
    pl.when((j == T_BQ) | (j == T_BQ + 1))(column_tile(rope_q, wrot_ref, brot_ref))
    pl.when(j == T_BKV)(column_tile(rope_k_and_v, wrot_ref, brot_ref))
    pl.when(j >= T_GATE_A)(column_tile(gate))


def _in_proj(x, w, b, w_rot, b_rot, q_gain, k_gain, cos, sin, seq, layer):
    tokens = x.shape[0]
    tm = SUPER_TILE
    assert seq % tm == 0
    pos_blocks = seq // tm
    row_spec = lambda width: pl.BlockSpec((tm, width), lambda i, j: (i, 0))
    pos_spec = pl.BlockSpec((tm, HEAD_DIM), lambda i, j: (i % pos_blocks, 0))
    vec_spec = pl.BlockSpec((None, 1, HEAD_DIM), lambda i, j: (layer, 0, 0))
    head_of_col = np.arange(COL_TILE) // HEAD_DIM
    head_average = jnp.asarray((head_of_col[:, None] == head_of_col[None, :]) / HEAD_DIM, dtype=BF16)
    return pl.pallas_call(
        _in_proj_kernel,
        grid=(tokens // tm, N_COL_TILES),
        in_specs=[
            row_spec(D_MODEL),
            pl.BlockSpec((None, None, D_MODEL, COL_TILE), lambda i, j: (layer, j, 0, 0)),
            pl.BlockSpec((None, 1, COL_TILE), lambda i, j: (layer, 0, j)),
            pl.BlockSpec((None, None, D_MODEL, COL_TILE),
                         lambda i, j: (layer, jnp.clip(j - T_BQ, 0, ROTARY_TILES - 1), 0, 0)),
            pl.BlockSpec((None, 1, COL_TILE), lambda i, j: (layer, 0, jnp.clip(j - T_BQ, 0, ROTARY_TILES - 1))),
            vec_spec, vec_spec, pos_spec, pos_spec,
            pl.BlockSpec((COL_TILE, COL_TILE), lambda i, j: (0, 0)),
        ],
        out_specs=pl.BlockSpec((tm, COL_TILE), lambda i, j: (i, j)),
        out_shape=jax.ShapeDtypeStruct((tokens, IN_WIDTH), BF16),
        scratch_shapes=[
            pltpu.VMEM((tm, D_MODEL), BF16),
            pltpu.VMEM((tm // IN_PROJ_CHUNK, COL_TILE // HEAD_DIM, IN_PROJ_CHUNK, HEAD_DIM), F32),
            pltpu.VMEM((tm // IN_PROJ_CHUNK, COL_TILE // HEAD_DIM, IN_PROJ_CHUNK, HEAD_DIM), F32),
        ],
        compiler_params=pltpu.CompilerParams(
            dimension_semantics=("arbitrary", "arbitrary"), vmem_limit_bytes=VMEM_LIMIT_BYTES),
        name="in_proj",
    )(x, w, b, w_rot, b_rot, q_gain, k_gain, cos, sin, head_average)


def _flash_kernel(q_ref, k_ref, v_ref, g_ref, o_ref, qt_ref, vt_ref, m_ref, l_ref, acc_ref, k2max_ref,
                  *, tq, tk, n_chunks):
    i = pl.program_id(2)

    @pl.when(i == 0)
    def _():
        def fill(c, k2max):
            start = pl.multiple_of(c * tk, tk)
            vt_ref[c] = v_ref[pl.ds(start, tk), :].astype(F32).T.astype(BF16)
            kf = k_ref[pl.ds(start, tk), :].astype(F32)
            k2 = jnp.max(jnp.sum(kf * kf, axis=-1, keepdims=True), axis=0, keepdims=True)
            return jnp.maximum(k2max, k2)

        k2max_ref[...] = lax.fori_loop(0, n_chunks, fill, jnp.zeros((1, 1), F32))

    q2 = jnp.zeros((1, 1), F32)
    for hd in range(B_GROUP):
        qt = q_ref[:, hd * HEAD_DIM:(hd + 1) * HEAD_DIM].T
        qt_ref[:, hd * tq:(hd + 1) * tq] = qt
        qf = qt.astype(F32)
        q2 = jnp.maximum(q2, jnp.max(jnp.sum(qf * qf, axis=0, keepdims=True), axis=1, keepdims=True))
    acc_ref[...] = jnp.zeros(acc_ref.shape, F32)
    l_ref[...] = jnp.zeros(l_ref.shape, F32)

    logits_bounded = (q2 * k2max_ref[...])[0, 0] <= LOGIT_BOUND * LOGIT_BOUND

    def scores(c):
        start = pl.multiple_of(c * tk, tk)
        return jnp.dot(k_ref[pl.ds(start, tk), :], qt_ref[...], preferred_element_type=F32)

    def sublane_partial_sums(p):
        return jnp.sum(p.reshape(tk // SUBLANES, SUBLANES, p.shape[1]), axis=0)

    def bounded_body(c, carry):
        p = jnp.exp2(scores(c))
        l_ref[...] += sublane_partial_sums(p)
        acc_ref[...] += jnp.dot(vt_ref[c], p.astype(BF16), preferred_element_type=F32)
        return carry

    def online_body(c, carry):
        st = scores(c)
        m_prev = m_ref[...]
        m_new = jnp.maximum(m_prev, jnp.max(st, axis=0, keepdims=True))
        alpha = jnp.exp2(m_prev - m_new)
        p = jnp.exp2(st - m_new)
        l_ref[...] = alpha * l_ref[...] + sublane_partial_sums(p)
        acc_ref[...] = alpha * acc_ref[...] + jnp.dot(vt_ref[c], p.astype(BF16), preferred_element_type=F32)
        m_ref[...] = m_new
        return carry

    @pl.when(logits_bounded)
    def _():
        lax.fori_loop(0, n_chunks, bounded_body, 0, unroll=2)

    @pl.when(jnp.logical_not(logits_bounded))
    def _():
        m_ref[...] = jnp.full(m_ref.shape, MASK_VALUE, F32)
        lax.fori_loop(0, n_chunks, online_body, 0, unroll=2)

    ot = acc_ref[...] / jnp.sum(l_ref[...], axis=0, keepdims=True)
    for hd in range(B_GROUP):
        sl = slice(hd * HEAD_DIM, (hd + 1) * HEAD_DIM)
        o_ref[:, sl] = (ot[:, hd * tq:(hd + 1) * tq].T * g_ref[:, sl].astype(F32)).astype(BF16)


def _flash_tiles(seq):
    for tq in FLASH_Q_ROWS_CHOICES:
        tk = FLASH_SCORE_ELEMS // (B_GROUP * tq)
        kv_bytes = 2 * 2 * seq * HEAD_DIM * 2 + seq * HEAD_DIM * 2
        block_bytes = 3 * 2 * tq * COL_TILE * 2
        state_bytes = B_GROUP * tq * (HEAD_DIM * (2 + 4) + SUBLANES * 4 + 4)
        score_bytes = FLASH_SCORE_ELEMS * (4 + 2)
        if seq % tq == 0 and kv_bytes + block_bytes + state_bytes + score_bytes <= VMEM_LIMIT_BYTES:
            return tq, tk
    raise ValueError(f"no flash tiling fits VMEM for sequence length {seq}")


def _gqa_flash(h3):
    batch, seq, _ = h3.shape
    tq, tk = _flash_tiles(seq)
    n_chunks = seq // tk
    kernel = functools.partial(_flash_kernel, tq=tq, tk=tk, n_chunks=n_chunks)
    return pl.pallas_call(
        kernel,
        grid=(batch, B_KV_HEADS, seq // tq),
        in_specs=[
            pl.BlockSpec((None, tq, COL_TILE), lambda b, g, i: (b, i, T_BQ + g)),
            pl.BlockSpec((None, seq, HEAD_DIM), lambda b, g, i: (b, 0, BK_COL128 + g)),
            pl.BlockSpec((None, seq, HEAD_DIM), lambda b, g, i: (b, 0, BV_COL128 + g)),
            pl.BlockSpec((None, tq, COL_TILE), lambda b, g, i: (b, i, T_BG + g)),
        ],
        out_specs=pl.BlockSpec((None, tq, COL_TILE), lambda b, g, i: (b, i, g)),
        out_shape=jax.ShapeDtypeStruct((batch, seq, B_HEADS * HEAD_DIM), BF16),
        scratch_shapes=[
            pltpu.VMEM((HEAD_DIM, B_GROUP * tq), BF16),
            pltpu.VMEM((n_chunks, HEAD_DIM, tk), BF16),
            pltpu.VMEM((1, B_GROUP * tq), F32),
            pltpu.VMEM((SUBLANES, B_GROUP * tq), F32),
            pltpu.VMEM((HEAD_DIM, B_GROUP * tq), F32),
            pltpu.VMEM((1, 1), F32),
        ],
        compiler_params=pltpu.CompilerParams(
            dimension_semantics=("arbitrary", "arbitrary", "arbitrary"), vmem_limit_bytes=VMEM_LIMIT_BYTES),
        name="gqa_flash",
    )(h3, h3, h3, h3)


def _dilated_kernel(q_ref, k_ref, v_ref, kp_ref, kn_ref, vp_ref, vn_ref, bias_ref, o_ref, lse_ref,
                    kw_ref, vw_ref, stage_ref, *, dil, length):
    n = SUPER_TILE // dil
    win = DIL_SUB + 2 * N_SIDE
    first_pos = pl.program_id(1) * n
    lanes_per_head = HEAD_DIM // A_HEADS_PER_GROUP
    whole_class_window = n == DIL_SUB

    def sub_block(c, a):
        r0 = pl.multiple_of(a * DIL_SUB, DIL_SUB)
        key_pos = first_pos + a * DIL_SUB - N_SIDE + lax.broadcasted_iota(jnp.int32, (1, win), 1)
        key_ok = (key_pos >= 0) & (key_pos < length)
        token_rows = pl.ds(r0, DIL_SUB) if dil == 1 else pl.ds(c + dil * r0, DIL_SUB, stride=dil)
        lse_cols = []
        for hd in range(A_HEADS_PER_GROUP):
            sl = slice(hd * HEAD_DIM, (hd + 1) * HEAD_DIM)
            q = q_ref[pl.ds(pl.multiple_of(c * n + r0, DIL_SUB), DIL_SUB), sl]
            if whole_class_window:
                rows = pl.ds(pl.multiple_of(c * n, n), n)
                k = jnp.concatenate([kp_ref[c, :, sl], k_ref[rows, sl], kn_ref[c, :, sl]], axis=0)
                v = jnp.concatenate([vp_ref[c, :, sl], v_ref[rows, sl], vn_ref[c, :, sl]], axis=0)
            else:
                k = kw_ref[c, pl.ds(r0, win), sl]
                v = vw_ref[c, pl.ds(r0, win), sl]
            s = lax.dot_general(q, k, (((1,), (1,)), ((), ())), preferred_element_type=F32) + bias_ref[hd]
            s = jnp.where(key_ok, s, MASK_VALUE)
            m = jnp.max(s, axis=-1, keepdims=True)
            p = jnp.exp(s - m)
            l = jnp.sum(p, axis=-1, keepdims=True)
            o = jnp.dot(p.astype(BF16), v, preferred_element_type=F32) / l
            if dil == 1:
                o_ref[token_rows, sl] = o.astype(BF16)
            else:
                stage_ref[hd, token_rows, :] = o
            lse_cols.append(jnp.broadcast_to(m + jnp.log(l), (DIL_SUB, lanes_per_head)))
        lse_ref[token_rows, :] = jnp.concatenate(lse_cols, axis=-1)

    for c in range(0 if whole_class_window else dil):
        rows = slice(c * n, (c + 1) * n)
        kw_ref[c, 0:N_SIDE, :] = kp_ref[c]
        kw_ref[c, N_SIDE:N_SIDE + n, :] = k_ref[rows, :]
        kw_ref[c, N_SIDE + n:, :] = kn_ref[c]
        vw_ref[c, 0:N_SIDE, :] = vp_ref[c]
        vw_ref[c, N_SIDE:N_SIDE + n, :] = v_ref[rows, :]
        vw_ref[c, N_SIDE + n:, :] = vn_ref[c]

    subs_per_class = n // DIL_SUB

    def one_sub_block(sb, carry):
        sub_block(sb // subs_per_class, sb % subs_per_class)
        return carry

    lax.fori_loop(0, SUPER_TILE // DIL_SUB, one_sub_block, 0, unroll=DIL_UNROLL)

    if dil > 1:
        for hd in range(A_HEADS_PER_GROUP):
            o_ref[:, hd * HEAD_DIM:(hd + 1) * HEAD_DIM] = stage_ref[hd].astype(BF16)


def _dilated_group(h3, bias, group, dil):
    batch, seq, _ = h3.shape
    tiles = seq // SUPER_TILE
    n = SUPER_TILE // dil
    halos = n // N_SIDE
    h6 = h3.reshape(batch, tiles, dil, halos, N_SIDE, IN_WIDTH)

    def cur(tile):
        return pl.BlockSpec((None, SUPER_TILE, COL_TILE), lambda b, t: (b, t, tile + group))

    def prev(tile):
        return pl.BlockSpec((None, None, dil, None, N_SIDE, COL_TILE),
                            lambda b, t: (b, jnp.maximum(t - 1, 0), 0, halos - 1, 0, tile + group))

    def nxt(tile):
        return pl.BlockSpec((None, None, dil, None, N_SIDE, COL_TILE),
                            lambda b, t: (b, jnp.minimum(t + 1, tiles - 1), 0, 0, 0, tile + group))

    kernel = functools.partial(_dilated_kernel, dil=dil, length=seq // dil)
    o, lse = pl.pallas_call(
        kernel,
        grid=(batch, tiles),
        in_specs=[
            cur(T_AQ), cur(T_AK), cur(T_AV), prev(T_AK), nxt(T_AK), prev(T_AV), nxt(T_AV),
            pl.BlockSpec((A_HEADS_PER_GROUP, DIL_SUB, DIL_SUB + 2 * N_SIDE), lambda b, t: (0, 0, 0)),
        ],
        out_specs=[
            pl.BlockSpec((None, SUPER_TILE, A_WIDTH), lambda b, t: (b, t, 0)),
            pl.BlockSpec((None, SUPER_TILE, HEAD_DIM), lambda b, t: (b, t, 0)),
        ],
        out_shape=[
            jax.ShapeDtypeStruct((batch, seq, A_WIDTH), BF16),
            jax.ShapeDtypeStruct((batch, seq, HEAD_DIM), F32),
        ],
        scratch_shapes=[
            pltpu.VMEM((dil, n + 2 * N_SIDE, COL_TILE), BF16),
            pltpu.VMEM((dil, n + 2 * N_SIDE, COL_TILE), BF16),
            pltpu.VMEM((A_HEADS_PER_GROUP, SUPER_TILE, HEAD_DIM), F32),
        ],
        compiler_params=pltpu.CompilerParams(
            dimension_semantics=("arbitrary", "arbitrary"), vmem_limit_bytes=VMEM_LIMIT_BYTES),
        name=f"dilated_g{group}",
    )(h3, h3, h3, h6, h6, h6, h6, bias)
    return o.reshape(batch * seq, A_WIDTH), lse.reshape(batch * seq, HEAD_DIM)


def _post_kernel(x_ref, o0_ref, o1_ref, o2_ref, l0_ref, l1_ref, l2_ref, ag_ref, yb_ref,
                 ga0_ref, ga1_ref, gb0_ref, gb1_ref, wpa_ref, wpb_ref, wo_ref, lng_ref, lnb_ref, y_ref):
    lanes_per_head = HEAD_DIM // A_HEADS_PER_GROUP
    for rc in range(x_ref.shape[0] // POST_CHUNK):
        rows = slice(rc * POST_CHUNK, (rc + 1) * POST_CHUNK)
        l0, l1, l2 = l0_ref[rows, :], l1_ref[rows, :], l2_ref[rows, :]
        top = jnp.maximum(jnp.maximum(l0, l1), l2)
        e0, e1, e2 = jnp.exp(l0 - top), jnp.exp(l1 - top), jnp.exp(l2 - top)
        inv = 1.0 / (e0 + e1 + e2)
        ya_heads = []
        for hd in range(A_HEADS_PER_GROUP):
            sl = slice(hd * HEAD_DIM, (hd + 1) * HEAD_DIM)
            pick = slice(hd * lanes_per_head, hd * lanes_per_head + 1)
            w0 = jnp.broadcast_to((e0 * inv)[:, pick], (POST_CHUNK, HEAD_DIM))
            w1 = jnp.broadcast_to((e1 * inv)[:, pick], (POST_CHUNK, HEAD_DIM))
            w2 = jnp.broadcast_to((e2 * inv)[:, pick], (POST_CHUNK, HEAD_DIM))
            mix = (w0 * o0_ref[rows, sl].astype(F32) + w1 * o1_ref[rows, sl].astype(F32)
                   + w2 * o2_ref[rows, sl].astype(F32))
            ya_heads.append((mix * ag_ref[rows, sl].astype(F32)).astype(BF16))
        ya = jnp.concatenate(ya_heads, axis=-1)
        pa = jnp.dot(ya, wpa_ref[...], preferred_element_type=F32)
        pb = jnp.dot(yb_ref[rows, :], wpb_ref[...], preferred_element_type=F32)
        gate_a = jnp.concatenate([ga0_ref[rows, :], ga1_ref[rows, :]], axis=-1).astype(F32)
        gate_b = jnp.concatenate([gb0_ref[rows, :], gb1_ref[rows, :]], axis=-1).astype(F32)
        merged = (gate_a * pa + gate_b * pb).astype(BF16)
        z = ALPHA * x_ref[rows, :] + jnp.dot(merged, wo_ref[...], preferred_element_type=F32)
        mu = jnp.mean(z, axis=-1, keepdims=True)
        zc = z - mu
        var = jnp.mean(zc * zc, axis=-1, keepdims=True)
        y_ref[rows, :] = zc * lax.rsqrt(var + LN_EPS) * lng_ref[...] + lnb_ref[...]


def _post(x, outs, lses, h, yb, wpa, wpb, wo, ln_g, ln_b, layer):
    tokens = x.shape[0]
    tm = POST_ROWS
    row = lambda width, col=0: pl.BlockSpec((tm, width), lambda i, col=col: (i, col))
    full = lambda a: pl.BlockSpec((None,) + a.shape[1:], lambda i: (layer, 0, 0), pipeline_mode=pl.Buffered(1))
    return pl.pallas_call(
        _post_kernel,
        grid=(tokens // tm,),
        in_specs=[
            row(D_MODEL),
            row(A_WIDTH), row(A_WIDTH), row(A_WIDTH),
            row(HEAD_DIM), row(HEAD_DIM), row(HEAD_DIM),
            row(COL_TILE, T_AG),
            row(B_HEADS * HEAD_DIM),
            row(COL_TILE, T_GATE_A), row(COL_TILE, T_GATE_A + 1),
            row(COL_TILE, T_GATE_B), row(COL_TILE, T_GATE_B + 1),
            full(wpa), full(wpb), full(wo), full(ln_g), full(ln_b),
        ],
        out_specs=row(D_MODEL),
        out_shape=jax.ShapeDtypeStruct((tokens, D_MODEL), F32),
        compiler_params=pltpu.CompilerParams(
            dimension_semantics=("arbitrary",), vmem_limit_bytes=VMEM_LIMIT_BYTES),
        name="post",
    )(x, *outs, *lses, h, yb, h, h, h, h, wpa, wpb, wo, ln_g, ln_b)


def _rope_tables(seq):
    rows = seq // GRID_W
    axis_dim = HEAD_DIM // 2
    inv_freq = ROPE_THETA ** (-jnp.arange(0, axis_dim, 2, dtype=F32) / axis_dim)
    ang_r = jnp.arange(rows).astype(F32)[:, None] * inv_freq[None]
    ang_c = jnp.arange(GRID_W).astype(F32)[:, None] * inv_freq[None]

    def over_positions(per_row, per_col):
        r = jnp.broadcast_to(per_row[:, None, :], (rows, GRID_W, axis_dim // 2))
        c = jnp.broadcast_to(per_col[None, :, :], (rows, GRID_W, axis_dim // 2))
        return r, c

    cos_r, cos_c = over_positions(jnp.cos(ang_r), jnp.cos(ang_c))
    sin_r, sin_c = over_positions(jnp.sin(ang_r), jnp.sin(ang_c))
    cos = jnp.concatenate([cos_r, cos_c, cos_r, cos_c], axis=-1).reshape(seq, HEAD_DIM)
    sin = jnp.concatenate([-sin_r, -sin_c, sin_r, sin_c], axis=-1).reshape(seq, HEAD_DIM)
    return cos, sin


def _pair_rotary_halves(a, first_col, n_heads):
    seg = a[..., first_col:first_col + n_heads * HEAD_DIM]
    seg = seg.reshape(*seg.shape[:-1], n_heads, 2, 2, HEAD_DIM // 4)
    seg = jnp.swapaxes(seg, -3, -2).reshape(*a.shape[:-1], n_heads * HEAD_DIM)
    return jnp.concatenate([a[..., :first_col], seg, a[..., first_col + n_heads * HEAD_DIM:]], axis=-1)


def _alibi_bias(group, dil):
    slopes = 2.0 ** (-ALIBI_MAX_BIAS * np.arange(1, A_HEADS + 1) / A_HEADS)
    slopes = slopes[group * A_HEADS_PER_GROUP:(group + 1) * A_HEADS_PER_GROUP]
    rel = np.arange(DIL_SUB + 2 * N_SIDE)[None, :] - N_SIDE - np.arange(DIL_SUB)[:, None]
    inside = np.abs(rel) <= N_SIDE
    bias = -slopes[:, None, None] * (dil * np.abs(rel)).astype(np.float32)[None]
    return jnp.asarray(np.where(inside[None], bias, MASK_VALUE), dtype=F32)


def _trunk(x, params, biases, cos, sin):
    batch, seq, _ = x.shape
    xf = x.reshape(batch * seq, D_MODEL)
    w_in, b_in, w_rot, b_rot, q_gain, k_gain, wpa, wpb, wo, ln_g, ln_b = params
    for layer in range(DEPTH):
        h = _in_proj(xf, w_in, b_in, w_rot, b_rot, q_gain, k_gain, cos, sin, seq, layer)
        h3 = h.reshape(batch, seq, IN_WIDTH)
        yb = _gqa_flash(h3).reshape(batch * seq, B_HEADS * HEAD_DIM)
        outs, lses = [], []
        for group, (_, dil) in enumerate(A_GROUPS):
            o, lse = _dilated_group(h3, biases[group], group, dil)
            outs.append(o)
            lses.append(lse)
        xf = _post(xf, outs, lses, h, yb, wpa, wpb, wo, ln_g, ln_b, layer)
    return xf.reshape(batch, seq, D_MODEL)


def _prepare_params(w_in, b_in, q_gain, k_gain, w_proj_a, w_proj_b, w_out, ln_g, ln_b):
    rotary_heads = B_HEADS + B_KV_HEADS
    rotary_cols = slice(T_BQ * COL_TILE, (T_BQ + ROTARY_TILES) * COL_TILE)

    def tile_major(w, tiles):
        return w.astype(BF16).reshape(DEPTH, D_MODEL, tiles, COL_TILE).transpose(0, 2, 1, 3)

    return (
        tile_major(w_in, N_COL_TILES),
        b_in.reshape(DEPTH, 1, IN_WIDTH),
        tile_major(_pair_rotary_halves(w_in[..., rotary_cols], 0, rotary_heads), ROTARY_TILES),
        _pair_rotary_halves(b_in[..., rotary_cols], 0, rotary_heads).reshape(DEPTH, 1, ROTARY_TILES * COL_TILE),
        _pair_rotary_halves(q_gain, 0, 1).reshape(DEPTH, 1, HEAD_DIM),
        _pair_rotary_halves(k_gain, 0, 1).reshape(DEPTH, 1, HEAD_DIM),
        w_proj_a.astype(BF16),
        w_proj_b.astype(BF16),
        w_out.astype(BF16),
        ln_g.reshape(DEPTH, 1, D_MODEL),
        ln_b.reshape(DEPTH, 1, D_MODEL),
    )


def kernel(x_prompt, x_sample, w_in, b_in, q_gain, k_gain, w_proj_a, w_proj_b, w_out, ln_g, ln_b):
    params = _prepare_params(w_in, b_in, q_gain, k_gain, w_proj_a, w_proj_b, w_out, ln_g, ln_b)
    biases = [_alibi_bias(group, dil) for group, (_, dil) in enumerate(A_GROUPS)]
    cos, sin = _rope_tables(max(x_prompt.shape[1], x_sample.shape[1]))
    return (_trunk(x_prompt, params, biases, cos, sin), _trunk(x_sample, params, biases, cos, sin))
```

```python
import functools
import math

import jax
import jax.numpy as jnp
import numpy as np
from jax import lax
from jax.experimental import pallas as pl
from jax.experimental.pallas import tpu as pltpu

F32 = jnp.float32
BF16 = jnp.bfloat16

D_MODEL = 1024
DEPTH = 2
HEAD_DIM = 128
GRID_W = 64
A_GROUPS = ((128, 1), (512, 4), (2048, 16))
A_HEADS_PER_GROUP = 4
A_HEADS = 12
A_WIDTH = 512
ALIBI_MAX_BIAS = 8.0
B_HEADS = 8
B_KV_HEADS = 2
B_GROUP = B_HEADS // B_KV_HEADS
ROPE_THETA = 10000.0
IN_WIDTH = 9728
RMS_EPS = 1e-6
LN_EPS = 1e-5
MASK_VALUE = -1e30
ALPHA = (2 * DEPTH) ** 0.25
QK_SCALE = 1.0 / math.sqrt(HEAD_DIM)
BQ_SCALE = QK_SCALE * math.log2(math.e)
SUBLANES = 8
MXU_WIDTH = 256
LOGIT_BOUND = 60.0

COL_TILE = 512
N_COL_TILES = IN_WIDTH // COL_TILE
T_AQ, T_AK, T_AV, T_AG = 0, 3, 6, 9
T_BQ, T_BKV, T_BG = 10, 12, 13
T_GATE_A, T_GATE_B = 15, 17
ROTARY_TILES = T_BKV + 1 - T_BQ
BK_COL128 = T_BKV * COL_TILE // HEAD_DIM
BV_COL128 = BK_COL128 + B_KV_HEADS

N_SIDE = 64

VMEM_LIMIT_BYTES = 56 * 1024 * 1024

SUPER_TILE = 2048
IN_PROJ_CHUNK = 512
GATHER_STRIDE = 4
FLASH_Q_ROWS_CHOICES = (1024, 512)
FLASH_SCORE_ELEMS = 4 * 1024 * 1024
DIL_SUB = 128
DIL_UNROLL = 8
POST_ROWS = 1024
POST_CHUNK = 512


def _sigmoid(x):
    return 0.5 * jnp.tanh(0.5 * x) + 0.5


def _in_proj_kernel(x_ref, w_ref, b_ref, wrot_ref, brot_ref, qg_ref, kg_ref, cos_ref, sin_ref, avg_ref,
                    o_ref, xb_ref, hs_ref, hs2_ref):
    j = pl.program_id(1)
    tm = x_ref.shape[0]

    @pl.when(j == 0)
    def _():
        xb_ref[...] = x_ref[...].astype(BF16)

    lane_tiles = COL_TILE // HEAD_DIM
    n_chunks = tm // IN_PROJ_CHUNK

    def column_tile(epilogue, weights=w_ref, bias=b_ref):
        def run():
            for rc in range(n_chunks):
                rows = slice(rc * IN_PROJ_CHUNK, (rc + 1) * IN_PROJ_CHUNK)
                h = jnp.dot(xb_ref[rows, :], weights[...], preferred_element_type=F32) + bias[...]
                epilogue(rc, rows, h)
        return run

    def head_mean_squares(h):
        hsq = (h * h).astype(BF16)
        return jnp.concatenate(
            [jnp.dot(hsq[:, c0:c0 + MXU_WIDTH], avg_ref[...], preferred_element_type=F32)
             for c0 in range(0, h.shape[1], MXU_WIDTH)], axis=-1)

    def rms_rope(hh, ms, gain, rows):
        y = hh * lax.rsqrt(ms + RMS_EPS) * gain
        return y * cos_ref[rows, :] + pltpu.roll(y, HEAD_DIM // 2, 1) * sin_ref[rows, :]

    a_scale = jnp.where(j < T_AK, QK_SCALE, 1.0)
    a_group = j % len(A_GROUPS)

    def plain(rc, rows, h):
        o_ref[rows, :] = (h * a_scale).astype(BF16)

    def class_major(dil):
        n = tm // dil
        per_chunk = IN_PROJ_CHUNK // dil

        def epilogue(rc, rows, h):
            for ln in range(lane_tiles):
                hs_ref[rc, ln] = h[:, ln * HEAD_DIM:(ln + 1) * HEAD_DIM] * a_scale
            src_ref, stride = hs_ref, dil
            if dil == GATHER_STRIDE * GATHER_STRIDE:
                quarter = IN_PROJ_CHUNK // GATHER_STRIDE
                for ln in range(lane_tiles):
                    for c4 in range(GATHER_STRIDE):
                        hs2_ref[rc, ln, c4 * quarter:(c4 + 1) * quarter, :] = (
                            hs_ref[rc, ln, pl.ds(c4, quarter, stride=GATHER_STRIDE), :])
                src_ref, stride = hs2_ref, GATHER_STRIDE
            for c in range(dil):
                if src_ref is hs_ref:
                    first = c
                else:
                    first = (c % GATHER_STRIDE) * (IN_PROJ_CHUNK // GATHER_STRIDE) + c // GATHER_STRIDE
                dst = slice(c * n + rc * per_chunk, c * n + (rc + 1) * per_chunk)
                for ln in range(lane_tiles):
                    o_ref[dst, ln * HEAD_DIM:(ln + 1) * HEAD_DIM] = (
                        src_ref[rc, ln, pl.ds(first, per_chunk, stride=stride), :].astype(BF16))
        return epilogue

    def silu(rc, rows, h):
        o_ref[rows, :] = (h * _sigmoid(h)).astype(BF16)

    def rope_q(rc, rows, h):
        ms = head_mean_squares(h)
        for hd in range(lane_tiles):
            sl = slice(hd * HEAD_DIM, (hd + 1) * HEAD_DIM)
            o_ref[rows, sl] = (rms_rope(h[:, sl], ms[:, sl], qg_ref[...], rows) * BQ_SCALE).astype(BF16)

    def rope_k_and_v(rc, rows, h):
        k_width = B_KV_HEADS * HEAD_DIM
        ms = head_mean_squares(h[:, :k_width])
        for hd in range(B_KV_HEADS):
            sl = slice(hd * HEAD_DIM, (hd + 1) * HEAD_DIM)
            o_ref[rows, sl] = rms_rope(h[:, sl], ms[:, sl], kg_ref[...], rows).astype(BF16)
        o_ref[rows, k_width:] = h[:, k_width:].astype(BF16)

    def gate(rc, rows, h):
        o_ref[rows, :] = _sigmoid(h).astype(BF16)

    for group, (_, dil) in enumerate(A_GROUPS):
        pl.when((j < T_AG) & (a_group == group))(column_tile(plain if dil == 1 else class_major(dil)))
    pl.when((j == T_AG) | (j == T_BG) | (j == T_BG + 1))(column_tile(silu))
    pl.when((j == T_BQ) | (j == T_BQ + 1))(column_tile(rope_q, wrot_ref, brot_ref))
    pl.when(j == T_BKV)(column_tile(rope_k_and_v, wrot_ref, brot_ref))
    pl.when(j >= T_GATE_A)(column_tile(gate))


def _in_proj(x, w, b, w_rot, b_rot, q_gain, k_gain, cos, sin, seq, layer):
    tokens = x.shape[0]
    tm = SUPER_TILE
    assert seq % tm == 0
    pos_blocks = seq // tm
    row_spec = lambda width: pl.BlockSpec((tm, width), lambda i, j: (i, 0))
    pos_spec = pl.BlockSpec((tm, HEAD_DIM), lambda i, j: (i % pos_blocks, 0))
    vec_spec = pl.BlockSpec((None, 1, HEAD_DIM), lambda i, j: (layer, 0, 0))
    head_of_col = np.arange(MXU_WIDTH) // HEAD_DIM
    head_average = jnp.asarray((head_of_col[:, None] == head_of_col[None, :]) / HEAD_DIM, dtype=BF16)
    return pl.pallas_call(
        _in_proj_kernel,
        grid=(tokens // tm, N_COL_TILES),
        in_specs=[
            row_spec(D_MODEL),
            pl.BlockSpec((None, D_MODEL, COL_TILE), lambda i, j: (layer, 0, j)),
            pl.BlockSpec((None, 1, COL_TILE), lambda i, j: (layer, 0, j)),
            pl.BlockSpec((None, D_MODEL, COL_TILE),
                         lambda i, j: (layer, 0, jnp.clip(j - T_BQ, 0, ROTARY_TILES - 1))),
            pl.BlockSpec((None, 1, COL_TILE), lambda i, j: (layer, 0, jnp.clip(j - T_BQ, 0, ROTARY_TILES - 1))),
            vec_spec, vec_spec, pos_spec, pos_spec,
            pl.BlockSpec((MXU_WIDTH, MXU_WIDTH), lambda i, j: (0, 0)),
        ],
        out_specs=pl.BlockSpec((tm, COL_TILE), lambda i, j: (i, j)),
        out_shape=jax.ShapeDtypeStruct((tokens, IN_WIDTH), BF16),
        scratch_shapes=[
            pltpu.VMEM((tm, D_MODEL), BF16),
            pltpu.VMEM((tm // IN_PROJ_CHUNK, COL_TILE // HEAD_DIM, IN_PROJ_CHUNK, HEAD_DIM), F32),
            pltpu.VMEM((tm // IN_PROJ_CHUNK, COL_TILE // HEAD_DIM, IN_PROJ_CHUNK, HEAD_DIM), F32),
        ],
        compiler_params=pltpu.CompilerParams(
            dimension_semantics=("arbitrary", "arbitrary"), vmem_limit_bytes=VMEM_LIMIT_BYTES),
        name="in_proj",
    )(x, w, b, w_rot, b_rot, q_gain, k_gain, cos, sin, head_average)


def _flash_kernel(q_ref, k_ref, v_ref, g_ref, o_ref, qt_ref, vt_ref, m_ref, l_ref, acc_ref, k2max_ref,
                  *, tq, tk, n_chunks):
    i = pl.program_id(2)

    @pl.when(i == 0)
    def _():
        def fill(c, k2max):
            start = pl.multiple_of(c * tk, tk)
            vt_ref[c] = v_ref[pl.ds(start, tk), :].astype(F32).T.astype(BF16)
            kf = k_ref[pl.ds(start, tk), :].astype(F32)
            k2 = jnp.max(jnp.sum(kf * kf, axis=-1, keepdims=True), axis=0, keepdims=True)
            return jnp.maximum(k2max, k2)

        k2max_ref[...] = lax.fori_loop(0, n_chunks, fill, jnp.zeros((1, 1), F32))

    q2 = jnp.zeros((1, 1), F32)
    for hd in range(B_GROUP):
        qt = q_ref[:, hd * HEAD_DIM:(hd + 1) * HEAD_DIM].T
        qt_ref[:, hd * tq:(hd + 1) * tq] = qt
        qf = qt.astype(F32)
        q2 = jnp.maximum(q2, jnp.max(jnp.sum(qf * qf, axis=0, keepdims=True), axis=1, keepdims=True))
    acc_ref[...] = jnp.zeros(acc_ref.shape, F32)
    l_ref[...] = jnp.zeros(l_ref.shape, F32)

    logits_bounded = (q2 * k2max_ref[...])[0, 0] <= LOGIT_BOUND * LOGIT_BOUND

    def scores(c):
        start = pl.multiple_of(c * tk, tk)
        return jnp.dot(k_ref[pl.ds(start, tk), :], qt_ref[...], preferred_element_type=F32)

    def sublane_partial_sums(p):
        return jnp.sum(p.reshape(tk // SUBLANES, SUBLANES, p.shape[1]), axis=0)

    def bounded_body(c, carry):
        p = jnp.exp2(scores(c))
        l_ref[...] += sublane_partial_sums(p)
        acc_ref[...] += jnp.dot(vt_ref[c], p.astype(BF16), preferred_element_type=F32)
        return carry

    def online_body(c, carry):
        st = scores(c)
        m_prev = m_ref[...]
        m_new = jnp.maximum(m_prev, jnp.max(st, axis=0, keepdims=True))
        alpha = jnp.exp2(m_prev - m_new)
        p = jnp.exp2(st - m_new)
        l_ref[...] = alpha * l_ref[...] + sublane_partial_sums(p)
        acc_ref[...] = alpha * acc_ref[...] + jnp.dot(vt_ref[c], p.astype(BF16), preferred_element_type=F32)
        m_ref[...] = m_new
        return carry

    @pl.when(logits_bounded)
    def _():
        lax.fori_loop(0, n_chunks, bounded_body, 0, unroll=2)

    @pl.when(jnp.logical_not(logits_bounded))
    def _():
        m_ref[...] = jnp.full(m_ref.shape, MASK_VALUE, F32)
        lax.fori_loop(0, n_chunks, online_body, 0, unroll=2)

    ot = acc_ref[...] / jnp.sum(l_ref[...], axis=0, keepdims=True)
    for hd in range(B_GROUP):
        sl = slice(hd * HEAD_DIM, (hd + 1) * HEAD_DIM)
        o_ref[:, sl] = (ot[:, hd * tq:(hd + 1) * tq].T * g_ref[:, sl].astype(F32)).astype(BF16)


def _flash_tiles(seq):
    for tq in FLASH_Q_ROWS_CHOICES:
        tk = FLASH_SCORE_ELEMS // (B_GROUP * tq)
        kv_bytes = 2 * 2 * seq * HEAD_DIM * 2 + seq * HEAD_DIM * 2
        block_bytes = 3 * 2 * tq * COL_TILE * 2
        state_bytes = B_GROUP * tq * (HEAD_DIM * (2 + 4) + SUBLANES * 4 + 4)
        score_bytes = FLASH_SCORE_ELEMS * (4 + 2)
        if seq % tq == 0 and kv_bytes + block_bytes + state_bytes + score_bytes <= VMEM_LIMIT_BYTES:
            return tq, tk
    raise ValueError(f"no flash tiling fits VMEM for sequence length {seq}")


def _gqa_flash(h3):
    batch, seq, _ = h3.shape
    tq, tk = _flash_tiles(seq)
    n_chunks = seq // tk
    kernel = functools.partial(_flash_kernel, tq=tq, tk=tk, n_chunks=n_chunks)
    return pl.pallas_call(
        kernel,
        grid=(batch, B_KV_HEADS, seq // tq),
        in_specs=[
            pl.BlockSpec((None, tq, COL_TILE), lambda b, g, i: (b, i, T_BQ + g)),
            pl.BlockSpec((None, seq, HEAD_DIM), lambda b, g, i: (b, 0, BK_COL128 + g)),
            pl.BlockSpec((None, seq, HEAD_DIM), lambda b, g, i: (b, 0, BV_COL128 + g)),
            pl.BlockSpec((None, tq, COL_TILE), lambda b, g, i: (b, i, T_BG + g)),
        ],
        out_specs=pl.BlockSpec((None, tq, COL_TILE), lambda b, g, i: (b, i, g)),
        out_shape=jax.ShapeDtypeStruct((batch, seq, B_HEADS * HEAD_DIM), BF16),
        scratch_shapes=[
            pltpu.VMEM((HEAD_DIM, B_GROUP * tq), BF16),
            pltpu.VMEM((n_chunks, HEAD_DIM, tk), BF16),
            pltpu.VMEM((1, B_GROUP * tq), F32),
            pltpu.VMEM((SUBLANES, B_GROUP * tq), F32),
            pltpu.VMEM((HEAD_DIM, B_GROUP * tq), F32),
            pltpu.VMEM((1, 1), F32),
        ],
        compiler_params=pltpu.CompilerParams(
            dimension_semantics=("arbitrary", "arbitrary", "arbitrary"), vmem_limit_bytes=VMEM_LIMIT_BYTES),
        name="gqa_flash",
    )(h3, h3, h3, h3)


def _dilated_kernel(q_ref, k_ref, v_ref, kp_ref, kn_ref, vp_ref, vn_ref, bias_ref, o_ref, lse_ref,
                    kw_ref, vw_ref, stage_ref, *, dil, length):
    n = SUPER_TILE // dil
    win = DIL_SUB + 2 * N_SIDE
    first_pos = pl.program_id(1) * n
    lanes_per_head = HEAD_DIM // A_HEADS_PER_GROUP
    whole_class_window = n == DIL_SUB

    def sub_block(c, a):
        r0 = pl.multiple_of(a * DIL_SUB, DIL_SUB)
        key_pos = first_pos + a * DIL_SUB - N_SIDE + lax.broadcasted_iota(jnp.int32, (1, win), 1)
        key_ok = (key_pos >= 0) & (key_pos < length)
        token_rows = pl.ds(r0, DIL_SUB) if dil == 1 else pl.ds(c + dil * r0, DIL_SUB, stride=dil)
        lse_cols = []
        for hd in range(A_HEADS_PER_GROUP):
            sl = slice(hd * HEAD_DIM, (hd + 1) * HEAD_DIM)
            q = q_ref[pl.ds(pl.multiple_of(c * n + r0, DIL_SUB), DIL_SUB), sl]
            if whole_class_window:
                rows = pl.ds(pl.multiple_of(c * n, n), n)
                k = jnp.concatenate([kp_ref[c, :, sl], k_ref[rows, sl], kn_ref[c, :, sl]], axis=0)
                v = jnp.concatenate([vp_ref[c, :, sl], v_ref[rows, sl], vn_ref[c, :, sl]], axis=0)
            else:
                k = kw_ref[c, pl.ds(r0, win), sl]
                v = vw_ref[c, pl.ds(r0, win), sl]
            s = lax.dot_general(q, k, (((1,), (1,)), ((), ())), preferred_element_type=F32) + bias_ref[hd]
            s = jnp.where(key_ok, s, MASK_VALUE)
            m = jnp.max(s, axis=-1, keepdims=True)
            p = jnp.exp(s - m)
            l = jnp.sum(p, axis=-1, keepdims=True)
            o = jnp.dot(p.astype(BF16), v, preferred_element_type=F32) / l
            if dil == 1:
                o_ref[token_rows, sl] = o.astype(BF16)
            else:
                stage_ref[hd, token_rows, :] = o
            lse_cols.append(jnp.broadcast_to(m + jnp.log(l), (DIL_SUB, lanes_per_head)))
        lse_ref[token_rows, :] = jnp.concatenate(lse_cols, axis=-1)

    for c in range(0 if whole_class_window else dil):
        rows = slice(c * n, (c + 1) * n)
        kw_ref[c, 0:N_SIDE, :] = kp_ref[c]
        kw_ref[c, N_SIDE:N_SIDE + n, :] = k_ref[rows, :]
        kw_ref[c, N_SIDE + n:, :] = kn_ref[c]
        vw_ref[c, 0:N_SIDE, :] = vp_ref[c]
        vw_ref[c, N_SIDE:N_SIDE + n, :] = v_ref[rows, :]
        vw_ref[c, N_SIDE + n:, :] = vn_ref[c]

    subs_per_class = n // DIL_SUB

    def one_sub_block(sb, carry):
        sub_block(sb // subs_per_class, sb % subs_per_class)
        return carry

    lax.fori_loop(0, SUPER_TILE // DIL_SUB, one_sub_block, 0, unroll=DIL_UNROLL)

    if dil > 1:
        for hd in range(A_HEADS_PER_GROUP):
            o_ref[:, hd * HEAD_DIM:(hd + 1) * HEAD_DIM] = stage_ref[hd].astype(BF16)


def _dilated_group(h3, bias, group, dil):
    batch, seq, _ = h3.shape
    tiles = seq // SUPER_TILE
    n = SUPER_TILE // dil
    halos = n // N_SIDE
    h6 = h3.reshape(batch, tiles, dil, halos, N_SIDE, IN_WIDTH)

    def cur(tile):
        return pl.BlockSpec((None, SUPER_TILE, COL_TILE), lambda b, t: (b, t, tile + group))

    def prev(tile):
        return pl.BlockSpec((None, None, dil, None, N_SIDE, COL_TILE),
                            lambda b, t: (b, jnp.maximum(t - 1, 0), 0, halos - 1, 0, tile + group))

    def nxt(tile):
        return pl.BlockSpec((None, None, dil, None, N_SIDE, COL_TILE),
                            lambda b, t: (b, jnp.minimum(t + 1, tiles - 1), 0, 0, 0, tile + group))

    kernel = functools.partial(_dilated_kernel, dil=dil, length=seq // dil)
    o, lse = pl.pallas_call(
        kernel,
        grid=(batch, tiles),
        in_specs=[
            cur(T_AQ), cur(T_AK), cur(T_AV), prev(T_AK), nxt(T_AK), prev(T_AV), nxt(T_AV),
            pl.BlockSpec((A_HEADS_PER_GROUP, DIL_SUB, DIL_SUB + 2 * N_SIDE), lambda b, t: (0, 0, 0)),
        ],
        out_specs=[
            pl.BlockSpec((None, SUPER_TILE, A_WIDTH), lambda b, t: (b, t, 0)),
            pl.BlockSpec((None, SUPER_TILE, HEAD_DIM), lambda b, t: (b, t, 0)),
        ],
        out_shape=[
            jax.ShapeDtypeStruct((batch, seq, A_WIDTH), BF16),
            jax.ShapeDtypeStruct((batch, seq, HEAD_DIM), F32),
        ],
        scratch_shapes=[
            pltpu.VMEM((dil, n + 2 * N_SIDE, COL_TILE), BF16),
            pltpu.VMEM((dil, n + 2 * N_SIDE, COL_TILE), BF16),
            pltpu.VMEM((A_HEADS_PER_GROUP, SUPER_TILE, HEAD_DIM), F32),
        ],
        compiler_params=pltpu.CompilerParams(
            dimension_semantics=("arbitrary", "arbitrary"), vmem_limit_bytes=VMEM_LIMIT_BYTES),
        name=f"dilated_g{group}",
    )(h3, h3, h3, h6, h6, h6, h6, bias)
    return o.reshape(batch * seq, A_WIDTH), lse.reshape(batch * seq, HEAD_DIM)


def _post_kernel(x_ref, o0_ref, o1_ref, o2_ref, l0_ref, l1_ref, l2_ref, ag_ref, yb_ref,
                 ga0_ref, ga1_ref, gb0_ref, gb1_ref, wpa_ref, wpb_ref, wo_ref, lng_ref, lnb_ref, y_ref):
    lanes_per_head = HEAD_DIM // A_HEADS_PER_GROUP
    for rc in range(x_ref.shape[0] // POST_CHUNK):
        rows = slice(rc * POST_CHUNK, (rc + 1) * POST_CHUNK)
        l0, l1, l2 = l0_ref[rows, :], l1_ref[rows, :], l2_ref[rows, :]
        top = jnp.maximum(jnp.maximum(l0, l1), l2)
        e0, e1, e2 = jnp.exp(l0 - top), jnp.exp(l1 - top), jnp.exp(l2 - top)
        inv = 1.0 / (e0 + e1 + e2)
        ya_heads = []
        for hd in range(A_HEADS_PER_GROUP):
            sl = slice(hd * HEAD_DIM, (hd + 1) * HEAD_DIM)
            pick = slice(hd * lanes_per_head, hd * lanes_per_head + 1)
            w0 = jnp.broadcast_to((e0 * inv)[:, pick], (POST_CHUNK, HEAD_DIM))
            w1 = jnp.broadcast_to((e1 * inv)[:, pick], (POST_CHUNK, HEAD_DIM))
            w2 = jnp.broadcast_to((e2 * inv)[:, pick], (POST_CHUNK, HEAD_DIM))
            mix = (w0 * o0_ref[rows, sl].astype(F32) + w1 * o1_ref[rows, sl].astype(F32)
                   + w2 * o2_ref[rows, sl].astype(F32))
            ya_heads.append((mix * ag_ref[rows, sl].astype(F32)).astype(BF16))
        ya = jnp.concatenate(ya_heads, axis=-1)
        pa = jnp.dot(ya, wpa_ref[...], preferred_element_type=F32)
        pb = jnp.dot(yb_ref[rows, :], wpb_ref[...], preferred_element_type=F32)
        gate_a = jnp.concatenate([ga0_ref[rows, :], ga1_ref[rows, :]], axis=-1).astype(F32)
        gate_b = jnp.concatenate([gb0_ref[rows, :], gb1_ref[rows, :]], axis=-1).astype(F32)
        merged = (gate_a * pa + gate_b * pb).astype(BF16)
        z = ALPHA * x_ref[rows, :] + jnp.dot(merged, wo_ref[...], preferred_element_type=F32)
        mu = jnp.mean(z, axis=-1, keepdims=True)
        zc = z - mu
        var = jnp.mean(zc * zc, axis=-1, keepdims=True)
        y_ref[rows, :] = zc * lax.rsqrt(var + LN_EPS) * lng_ref[...] + lnb_ref[...]


def _post(x, outs, lses, h, yb, wpa, wpb, wo, ln_g, ln_b, layer):
    tokens = x.shape[0]
    tm = POST_ROWS
    row = lambda width, col=0: pl.BlockSpec((tm, width), lambda i, col=col: (i, col))
    full = lambda a: pl.BlockSpec((None,) + a.shape[1:], lambda i: (layer, 0, 0), pipeline_mode=pl.Buffered(1))
    return pl.pallas_call(
        _post_kernel,
        grid=(tokens // tm,),
        in_specs=[
            row(D_MODEL),
            row(A_WIDTH), row(A_WIDTH), row(A_WIDTH),
            row(HEAD_DIM), row(HEAD_DIM), row(HEAD_DIM),
            row(COL_TILE, T_AG),
            row(B_HEADS * HEAD_DIM),
            row(COL_TILE, T_GATE_A), row(COL_TILE, T_GATE_A + 1),
            row(COL_TILE, T_GATE_B), row(COL_TILE, T_GATE_B + 1),
            full(wpa), full(wpb), full(wo), full(ln_g), full(ln_b),
        ],
        out_specs=row(D_MODEL),
        out_shape=jax.ShapeDtypeStruct((tokens, D_MODEL), F32),
        compiler_params=pltpu.CompilerParams(
            dimension_semantics=("arbitrary",), vmem_limit_bytes=VMEM_LIMIT_BYTES),
        name="post",
    )(x, *outs, *lses, h, yb, h, h, h, h, wpa, wpb, wo, ln_g, ln_b)


def _rope_tables(seq):
    rows = seq // GRID_W
    axis_dim = HEAD_DIM // 2
    inv_freq = ROPE_THETA ** (-jnp.arange(0, axis_dim, 2, dtype=F32) / axis_dim)
    ang_r = jnp.arange(rows).astype(F32)[:, None] * inv_freq[None]
    ang_c = jnp.arange(GRID_W).astype(F32)[:, None] * inv_freq[None]

    def over_positions(per_row, per_col):
        r = jnp.broadcast_to(per_row[:, None, :], (rows, GRID_W, axis_dim // 2))
        c = jnp.broadcast_to(per_col[None, :, :], (rows, GRID_W, axis_dim // 2))
        return r, c

    cos_r, cos_c = over_positions(jnp.cos(ang_r), jnp.cos(ang_c))
    sin_r, sin_c = over_positions(jnp.sin(ang_r), jnp.sin(ang_c))
    cos = jnp.concatenate([cos_r, cos_c, cos_r, cos_c], axis=-1).reshape(seq, HEAD_DIM)
    sin = jnp.concatenate([-sin_r, -sin_c, sin_r, sin_c], axis=-1).reshape(seq, HEAD_DIM)
    return cos, sin


def _pair_rotary_halves(a, first_col, n_heads):
    seg = a[..., first_col:first_col + n_heads * HEAD_DIM]
    seg = seg.reshape(*seg.shape[:-1], n_heads, 2, 2, HEAD_DIM // 4)
    seg = jnp.swapaxes(seg, -3, -2).reshape(*a.shape[:-1], n_heads * HEAD_DIM)
    return jnp.concatenate([a[..., :first_col], seg, a[..., first_col + n_heads * HEAD_DIM:]], axis=-1)


def _alibi_bias(group, dil):
    slopes = 2.0 ** (-ALIBI_MAX_BIAS * np.arange(1, A_HEADS + 1) / A_HEADS)
    slopes = slopes[group * A_HEADS_PER_GROUP:(group + 1) * A_HEADS_PER_GROUP]
    rel = np.arange(DIL_SUB + 2 * N_SIDE)[None, :] - N_SIDE - np.arange(DIL_SUB)[:, None]
    inside = np.abs(rel) <= N_SIDE
    bias = -slopes[:, None, None] * (dil * np.abs(rel)).astype(np.float32)[None]
    return jnp.asarray(np.where(inside[None], bias, MASK_VALUE), dtype=F32)


def _trunk(x, params, biases, cos, sin):
    batch, seq, _ = x.shape
    xf = x.reshape(batch * seq, D_MODEL)
    w_in, b_in, w_rot, b_rot, q_gain, k_gain, wpa, wpb, wo, ln_g, ln_b = params
    for layer in range(DEPTH):
        h = _in_proj(xf, w_in, b_in, w_rot, b_rot, q_gain, k_gain, cos, sin, seq, layer)
        h3 = h.reshape(batch, seq, IN_WIDTH)
        yb = _gqa_flash(h3).reshape(batch * seq, B_HEADS * HEAD_DIM)
        outs, lses = [], []
        for group, (_, dil) in enumerate(A_GROUPS):
            o, lse = _dilated_group(h3, biases[group], group, dil)
            outs.append(o)
            lses.append(lse)
        xf = _post(xf, outs, lses, h, yb, wpa, wpb, wo, ln_g, ln_b, layer)
    return xf.reshape(batch, seq, D_MODEL)


def _prepare_params(w_in, b_in, q_gain, k_gain, w_proj_a, w_proj_b, w_out, ln_g, ln_b):
    rotary_heads = B_HEADS + B_KV_HEADS
    rotary_cols = slice(T_BQ * COL_TILE, (T_BQ + ROTARY_TILES) * COL_TILE)

    return (
        w_in.astype(BF16),
        b_in.reshape(DEPTH, 1, IN_WIDTH),
        _pair_rotary_halves(w_in[..., rotary_cols], 0, rotary_heads).astype(BF16),
        _pair_rotary_halves(b_in[..., rotary_cols], 0, rotary_heads).reshape(DEPTH, 1, ROTARY_TILES * COL_TILE),
        _pair_rotary_halves(q_gain, 0, 1).reshape(DEPTH, 1, HEAD_DIM),
        _pair_rotary_halves(k_gain, 0, 1).reshape(DEPTH, 1, HEAD_DIM),
        w_proj_a.astype(BF16),
        w_proj_b.astype(BF16),
        w_out.astype(BF16),
        ln_g.reshape(DEPTH, 1, D_MODEL),
        ln_b.reshape(DEPTH, 1, D_MODEL),
    )


def kernel(x_prompt, x_sample, w_in, b_in, q_gain, k_gain, w_proj_a, w_proj_b, w_out, ln_g, ln_b):
    params = _prepare_params(w_in, b_in, q_gain, k_gain, w_proj_a, w_proj_b, w_out, ln_g, ln_b)
    biases = [_alibi_bias(group, dil) for group, (_, dil) in enumerate(A_GROUPS)]
    cos, sin = _rope_tables(max(x_prompt.shape[1], x_sample.shape[1]))
    return (_trunk(x_prompt, params, biases, cos, sin), _trunk(x_sample, params, biases, cos, sin))
```

```python
import functools
import math

import jax
import jax.numpy as jnp
import numpy as np
from jax import lax
from jax.experimental import pallas as pl
from jax.experimental.pallas import tpu as pltpu

F32 = jnp.float32
BF16 = jnp.bfloat16

D_MODEL = 1024
DEPTH = 2
HEAD_DIM = 128
GRID_W = 64
A_GROUPS = ((128, 1), (512, 4), (2048, 16))
A_HEADS_PER_GROUP = 4
A_HEADS = 12
A_WIDTH = 512
ALIBI_MAX_BIAS = 8.0
B_HEADS = 8
B_KV_HEADS = 2
B_GROUP = B_HEADS // B_KV_HEADS
ROPE_THETA = 10000.0
IN_WIDTH = 9728
RMS_EPS = 1e-6
LN_EPS = 1e-5
MASK_VALUE = -1e30
ALPHA = (2 * DEPTH) ** 0.25
QK_SCALE = 1.0 / math.sqrt(HEAD_DIM)
BQ_SCALE = QK_SCALE * math.log2(math.e)
SUBLANES = 8
MXU_WIDTH = 256
LOGIT_BOUND = 60.0

COL_TILE = 512
N_COL_TILES = IN_WIDTH // COL_TILE
T_AQ, T_AK, T_AV, T_AG = 0, 3, 6, 9
T_BQ, T_BKV, T_BG = 10, 12, 13
T_GATE_A, T_GATE_B = 15, 17
ROTARY_TILES = T_BKV + 1 - T_BQ
BK_COL128 = T_BKV * COL_TILE // HEAD_DIM
BV_COL128 = BK_COL128 + B_KV_HEADS

N_SIDE = 64

VMEM_LIMIT_BYTES = 56 * 1024 * 1024

SUPER_TILE = 2048
IN_PROJ_CHUNK = 512
GATHER_STRIDE = 4
FLASH_Q_ROWS_CHOICES = (1024, 512)
FLASH_SCORE_ELEMS = 4 * 1024 * 1024
DIL_SUB = 128
DIL_UNROLL = 16
POST_ROWS = 1024
POST_CHUNK = 512


def _sigmoid(x):
    return 0.5 * jnp.tanh(0.5 * x) + 0.5


def _in_proj_kernel(x_ref, w_ref, b_ref, wrot_ref, brot_ref, qg_ref, kg_ref, cos_ref, sin_ref, avg_ref,
                    o_ref, xb_ref, hs_ref, hs2_ref):
    j = pl.program_id(1)
    tm = x_ref.shape[0]

    @pl.when(j == 0)
    def _():
        xb_ref[...] = x_ref[...].astype(BF16)

    lane_tiles = COL_TILE // HEAD_DIM
    n_chunks = tm // IN_PROJ_CHUNK

    def column_tile(epilogue, weights=w_ref, bias=b_ref):
        def run():
            for rc in range(n_chunks):
                rows = slice(rc * IN_PROJ_CHUNK, (rc + 1) * IN_PROJ_CHUNK)
                h = jnp.dot(xb_ref[rows, :], weights[...], preferred_element_type=F32) + bias[...]
                epilogue(rc, rows, h)
        return run

    def head_mean_squares(h):
        hsq = (h * h).astype(BF16)
        return jnp.concatenate(
            [jnp.dot(hsq[:, c0:c0 + MXU_WIDTH], avg_ref[...], preferred_element_type=F32)
             for c0 in range(0, h.shape[1], MXU_WIDTH)], axis=-1)

    def rms_rope(hh, ms, gain, rows):
        y = hh * lax.rsqrt(ms + RMS_EPS) * gain
        return y * cos_ref[rows, :] + pltpu.roll(y, HEAD_DIM // 2, 1) * sin_ref[rows, :]

    a_scale = jnp.where(j < T_AK, QK_SCALE, 1.0)
    a_group = j % len(A_GROUPS)

    def plain(rc, rows, h):
        o_ref[rows, :] = (h * a_scale).astype(BF16)

    def class_major(dil):
        n = tm // dil
        per_chunk = IN_PROJ_CHUNK // dil

        def epilogue(rc, rows, h):
            for ln in range(lane_tiles):
                hs_ref[rc, ln] = h[:, ln * HEAD_DIM:(ln + 1) * HEAD_DIM] * a_scale
            src_ref, stride = hs_ref, dil
            if dil == GATHER_STRIDE * GATHER_STRIDE:
                quarter = IN_PROJ_CHUNK // GATHER_STRIDE
                for ln in range(lane_tiles):
                    for c4 in range(GATHER_STRIDE):
                        hs2_ref[rc, ln, c4 * quarter:(c4 + 1) * quarter, :] = (
                            hs_ref[rc, ln, pl.ds(c4, quarter, stride=GATHER_STRIDE), :])
                src_ref, stride = hs2_ref, GATHER_STRIDE
            for c in range(dil):
                if src_ref is hs_ref:
                    first = c
                else:
                    first = (c % GATHER_STRIDE) * (IN_PROJ_CHUNK // GATHER_STRIDE) + c // GATHER_STRIDE
                dst = slice(c * n + rc * per_chunk, c * n + (rc + 1) * per_chunk)
                for ln in range(lane_tiles):
                    o_ref[dst, ln * HEAD_DIM:(ln + 1) * HEAD_DIM] = (
                        src_ref[rc, ln, pl.ds(first, per_chunk, stride=stride), :].astype(BF16))
        return epilogue

    def silu(rc, rows, h):
        o_ref[rows, :] = (h * _sigmoid(h)).astype(BF16)

    def rope_q(rc, rows, h):
        ms = head_mean_squares(h)
        for hd in range(lane_tiles):
            sl = slice(hd * HEAD_DIM, (hd + 1) * HEAD_DIM)
            o_ref[rows, sl] = (rms_rope(h[:, sl], ms[:, sl], qg_ref[...], rows) * BQ_SCALE).astype(BF16)

    def rope_k_and_v(rc, rows, h):
        k_width = B_KV_HEADS * HEAD_DIM
        ms = head_mean_squares(h[:, :k_width])
        for hd in range(B_KV_HEADS):
            sl = slice(hd * HEAD_DIM, (hd + 1) * HEAD_DIM)
            o_ref[rows, sl] = rms_rope(h[:, sl], ms[:, sl], kg_ref[...], rows).astype(BF16)
        o_ref[rows, k_width:] = h[:, k_width:].astype(BF16)

    def gate(rc, rows, h):
        o_ref[rows, :] = _sigmoid(h).astype(BF16)

    for group, (_, dil) in enumerate(A_GROUPS):
        pl.when((j < T_AG) & (a_group == group))(column_tile(plain if dil == 1 else class_major(dil)))
    pl.when((j == T_AG) | (j == T_BG) | (j == T_BG + 1))(column_tile(silu))
    pl.when((j == T_BQ) | (j == T_BQ + 1))(column_tile(rope_q, wrot_ref, brot_ref))
    pl.when(j == T_BKV)(column_tile(rope_k_and_v, wrot_ref, brot_ref))
    pl.when(j >= T_GATE_A)(column_tile(gate))


def _in_proj(x, w, b, w_rot, b_rot, q_gain, k_gain, cos, sin, seq, layer):
    tokens = x.shape[0]
    tm = SUPER_TILE
    assert seq % tm == 0
    pos_blocks = seq // tm
    row_spec = lambda width: pl.BlockSpec((tm, width), lambda i, j: (i, 0))
    pos_spec = pl.BlockSpec((tm, HEAD_DIM), lambda i, j: (i % pos_blocks, 0))
    vec_spec = pl.BlockSpec((None, 1, HEAD_DIM), lambda i, j: (layer, 0, 0))
    head_of_col = np.arange(MXU_WIDTH) // HEAD_DIM
    head_average = jnp.asarray((head_of_col[:, None] == head_of_col[None, :]) / HEAD_DIM, dtype=BF16)
    return pl.pallas_call(
        _in_proj_kernel,
        grid=(tokens // tm, N_COL_TILES),
        in_specs=[
            row_spec(D_MODEL),
            pl.BlockSpec((None, D_MODEL, COL_TILE), lambda i, j: (layer, 0, j)),
            pl.BlockSpec((None, 1, COL_TILE), lambda i, j: (layer, 0, j)),
            pl.BlockSpec((None, D_MODEL, COL_TILE),
                         lambda i, j: (layer, 0, jnp.clip(j - T_BQ, 0, ROTARY_TILES - 1))),
            pl.BlockSpec((None, 1, COL_TILE), lambda i, j: (layer, 0, jnp.clip(j - T_BQ, 0, ROTARY_TILES - 1))),
            vec_spec, vec_spec, pos_spec, pos_spec,
            pl.BlockSpec((MXU_WIDTH, MXU_WIDTH), lambda i, j: (0, 0)),
        ],
        out_specs=pl.BlockSpec((tm, COL_TILE), lambda i, j: (i, j)),
        out_shape=jax.ShapeDtypeStruct((tokens, IN_WIDTH), BF16),
        scratch_shapes=[
            pltpu.VMEM((tm, D_MODEL), BF16),
            pltpu.VMEM((tm // IN_PROJ_CHUNK, COL_TILE // HEAD_DIM, IN_PROJ_CHUNK, HEAD_DIM), F32),
            pltpu.VMEM((tm // IN_PROJ_CHUNK, COL_TILE // HEAD_DIM, IN_PROJ_CHUNK, HEAD_DIM), F32),
        ],
        compiler_params=pltpu.CompilerParams(
            dimension_semantics=("arbitrary", "arbitrary"), vmem_limit_bytes=VMEM_LIMIT_BYTES),
        name="in_proj",
    )(x, w, b, w_rot, b_rot, q_gain, k_gain, cos, sin, head_average)


def _flash_kernel(q_ref, k_ref, v_ref, g_ref, o_ref, qt_ref, vt_ref, m_ref, l_ref, acc_ref, k2max_ref,
                  *, tq, tk, n_chunks):
    i = pl.program_id(2)

    @pl.when(i == 0)
    def _():
        def fill(c, k2max):
            start = pl.multiple_of(c * tk, tk)
            vt_ref[c] = v_ref[pl.ds(start, tk), :].astype(F32).T.astype(BF16)
            kf = k_ref[pl.ds(start, tk), :].astype(F32)
            k2 = jnp.max(jnp.sum(kf * kf, axis=-1, keepdims=True), axis=0, keepdims=True)
            return jnp.maximum(k2max, k2)

        k2max_ref[...] = lax.fori_loop(0, n_chunks, fill, jnp.zeros((1, 1), F32))

    q2 = jnp.zeros((1, 1), F32)
    for hd in range(B_GROUP):
        qt = q_ref[:, hd * HEAD_DIM:(hd + 1) * HEAD_DIM].T
        qt_ref[:, hd * tq:(hd + 1) * tq] = qt
        qf = qt.astype(F32)
        q2 = jnp.maximum(q2, jnp.max(jnp.sum(qf * qf, axis=0, keepdims=True), axis=1, keepdims=True))
    acc_ref[...] = jnp.zeros(acc_ref.shape, F32)
    l_ref[...] = jnp.zeros(l_ref.shape, F32)

    logits_bounded = (q2 * k2max_ref[...])[0, 0] <= LOGIT_BOUND * LOGIT_BOUND

    def scores(c):
        start = pl.multiple_of(c * tk, tk)
        return jnp.dot(k_ref[pl.ds(start, tk), :], qt_ref[...], preferred_element_type=F32)

    def sublane_partial_sums(p):
        return jnp.sum(p.reshape(tk // SUBLANES, SUBLANES, p.shape[1]), axis=0)

    def bounded_body(c, carry):
        p = jnp.exp2(scores(c))
        l_ref[...] += sublane_partial_sums(p)
        acc_ref[...] += jnp.dot(vt_ref[c], p.astype(BF16), preferred_element_type=F32)
        return carry

    def online_body(c, carry):
        st = scores(c)
        m_prev = m_ref[...]
        m_new = jnp.maximum(m_prev, jnp.max(st, axis=0, keepdims=True))
        alpha = jnp.exp2(m_prev - m_new)
        p = jnp.exp2(st - m_new)
        l_ref[...] = alpha * l_ref[...] + sublane_partial_sums(p)
        acc_ref[...] = alpha * acc_ref[...] + jnp.dot(vt_ref[c], p.astype(BF16), preferred_element_type=F32)
        m_ref[...] = m_new
        return carry

    @pl.when(logits_bounded)
    def _():
        lax.fori_loop(0, n_chunks, bounded_body, 0, unroll=2)

    @pl.when(jnp.logical_not(logits_bounded))
    def _():
        m_ref[...] = jnp.full(m_ref.shape, MASK_VALUE, F32)
        lax.fori_loop(0, n_chunks, online_body, 0, unroll=2)

    ot = acc_ref[...] / jnp.sum(l_ref[...], axis=0, keepdims=True)
    for hd in range(B_GROUP):
        sl = slice(hd * HEAD_DIM, (hd + 1) * HEAD_DIM)
        o_ref[:, sl] = (ot[:, hd * tq:(hd + 1) * tq].T * g_ref[:, sl].astype(F32)).astype(BF16)


def _flash_tiles(seq):
    for tq in FLASH_Q_ROWS_CHOICES:
        tk = FLASH_SCORE_ELEMS // (B_GROUP * tq)
        kv_bytes = 2 * 2 * seq * HEAD_DIM * 2 + seq * HEAD_DIM * 2
        block_bytes = 3 * 2 * tq * COL_TILE * 2
        state_bytes = B_GROUP * tq * (HEAD_DIM * (2 + 4) + SUBLANES * 4 + 4)
        score_bytes = FLASH_SCORE_ELEMS * (4 + 2)
        if seq % tq == 0 and kv_bytes + block_bytes + state_bytes + score_bytes <= VMEM_LIMIT_BYTES:
            return tq, tk
    raise ValueError(f"no flash tiling fits VMEM for sequence length {seq}")


def _gqa_flash(h3):
    batch, seq, _ = h3.shape
    tq, tk = _flash_tiles(seq)
    n_chunks = seq // tk
    kernel = functools.partial(_flash_kernel, tq=tq, tk=tk, n_chunks=n_chunks)
    return pl.pallas_call(
        kernel,
        grid=(batch, B_KV_HEADS, seq // tq),
        in_specs=[
            pl.BlockSpec((None, tq, COL_TILE), lambda b, g, i: (b, i, T_BQ + g)),
            pl.BlockSpec((None, seq, HEAD_DIM), lambda b, g, i: (b, 0, BK_COL128 + g)),
            pl.BlockSpec((None, seq, HEAD_DIM), lambda b, g, i: (b, 0, BV_COL128 + g)),
            pl.BlockSpec((None, tq, COL_TILE), lambda b, g, i: (b, i, T_BG + g)),
        ],
        out_specs=pl.BlockSpec((None, tq, COL_TILE), lambda b, g, i: (b, i, g)),
        out_shape=jax.ShapeDtypeStruct((batch, seq, B_HEADS * HEAD_DIM), BF16),
        scratch_shapes=[
            pltpu.VMEM((HEAD_DIM, B_GROUP * tq), BF16),
            pltpu.VMEM((n_chunks, HEAD_DIM, tk), BF16),
            pltpu.VMEM((1, B_GROUP * tq), F32),
            pltpu.VMEM((SUBLANES, B_GROUP * tq), F32),
            pltpu.VMEM((HEAD_DIM, B_GROUP * tq), F32),
            pltpu.VMEM((1, 1), F32),
        ],
        compiler_params=pltpu.CompilerParams(
            dimension_semantics=("arbitrary", "arbitrary", "arbitrary"), vmem_limit_bytes=VMEM_LIMIT_BYTES),
        name="gqa_flash",
    )(h3, h3, h3, h3)


def _dilated_kernel(q_ref, k_ref, v_ref, kp_ref, kn_ref, vp_ref, vn_ref, bias_ref, o_ref, lse_ref,
                    kw_ref, vw_ref, stage_ref, *, dil, length):
    n = SUPER_TILE // dil
    win = DIL_SUB + 2 * N_SIDE
    first_pos = pl.program_id(1) * n
    lanes_per_head = HEAD_DIM // A_HEADS_PER_GROUP
    whole_class_window = n == DIL_SUB

    def sub_block(c, a):
        r0 = pl.multiple_of(a * DIL_SUB, DIL_SUB)
        key_pos = first_pos + a * DIL_SUB - N_SIDE + lax.broadcasted_iota(jnp.int32, (1, win), 1)
        key_ok = (key_pos >= 0) & (key_pos < length)
        token_rows = pl.ds(r0, DIL_SUB) if dil == 1 else pl.ds(c + dil * r0, DIL_SUB, stride=dil)
        lse_cols = []
        for hd in range(A_HEADS_PER_GROUP):
            sl = slice(hd * HEAD_DIM, (hd + 1) * HEAD_DIM)
            q = q_ref[pl.ds(pl.multiple_of(c * n + r0, DIL_SUB), DIL_SUB), sl]
            if whole_class_window:
                rows = pl.ds(pl.multiple_of(c * n, n), n)
                k = jnp.concatenate([kp_ref[c, :, sl], k_ref[rows, sl], kn_ref[c, :, sl]], axis=0)
                v = jnp.concatenate([vp_ref[c, :, sl], v_ref[rows, sl], vn_ref[c, :, sl]], axis=0)
            else:
                k = kw_ref[c, pl.ds(r0, win), sl]
                v = vw_ref[c, pl.ds(r0, win), sl]
            s = lax.dot_general(q, k, (((1,), (1,)), ((), ())), preferred_element_type=F32) + bias_ref[hd]
            s = jnp.where(key_ok, s, MASK_VALUE)
            m = jnp.max(s, axis=-1, keepdims=True)
            p = jnp.exp(s - m)
            l = jnp.sum(p, axis=-1, keepdims=True)
            o = jnp.dot(p.astype(BF16), v, preferred_element_type=F32) / l
            if dil == 1:
                o_ref[token_rows, sl] = o.astype(BF16)
            else:
                stage_ref[hd, token_rows, :] = o
            lse_cols.append(jnp.broadcast_to(m + jnp.log(l), (DIL_SUB, lanes_per_head)))
        lse_ref[token_rows, :] = jnp.concatenate(lse_cols, axis=-1)

    for c in range(0 if whole_class_window else dil):
        rows = slice(c * n, (c + 1) * n)
        kw_ref[c, 0:N_SIDE, :] = kp_ref[c]
        kw_ref[c, N_SIDE:N_SIDE + n, :] = k_ref[rows, :]
        kw_ref[c, N_SIDE + n:, :] = kn_ref[c]
        vw_ref[c, 0:N_SIDE, :] = vp_ref[c]
        vw_ref[c, N_SIDE:N_SIDE + n, :] = v_ref[rows, :]
        vw_ref[c, N_SIDE + n:, :] = vn_ref[c]

    subs_per_class = n // DIL_SUB

    def one_sub_block(sb, carry):
        sub_block(sb // subs_per_class, sb % subs_per_class)
        return carry

    lax.fori_loop(0, SUPER_TILE // DIL_SUB, one_sub_block, 0, unroll=DIL_UNROLL)

    if dil > 1:
        for hd in range(A_HEADS_PER_GROUP):
            o_ref[:, hd * HEAD_DIM:(hd + 1) * HEAD_DIM] = stage_ref[hd].astype(BF16)


def _dilated_group(h3, bias, group, dil):
    batch, seq, _ = h3.shape
    tiles = seq // SUPER_TILE
    n = SUPER_TILE // dil
    halos = n // N_SIDE
    h6 = h3.reshape(batch, tiles, dil, halos, N_SIDE, IN_WIDTH)

    def cur(tile):
        return pl.BlockSpec((None, SUPER_TILE, COL_TILE), lambda b, t: (b, t, tile + group))

    def prev(tile):
        return pl.BlockSpec((None, None, dil, None, N_SIDE, COL_TILE),
                            lambda b, t: (b, jnp.maximum(t - 1, 0), 0, halos - 1, 0, tile + group))

    def nxt(tile):
        return pl.BlockSpec((None, None, dil, None, N_SIDE, COL_TILE),
                            lambda b, t: (b, jnp.minimum(t + 1, tiles - 1), 0, 0, 0, tile + group))

    kernel = functools.partial(_dilated_kernel, dil=dil, length=seq // dil)
    o, lse = pl.pallas_call(
        kernel,
        grid=(batch, tiles),
        in_specs=[
            cur(T_AQ), cur(T_AK), cur(T_AV), prev(T_AK), nxt(T_AK), prev(T_AV), nxt(T_AV),
            pl.BlockSpec((A_HEADS_PER_GROUP, DIL_SUB, DIL_SUB + 2 * N_SIDE), lambda b, t: (0, 0, 0)),
        ],
        out_specs=[
            pl.BlockSpec((None, SUPER_TILE, A_WIDTH), lambda b, t: (b, t, 0)),
            pl.BlockSpec((None, SUPER_TILE, HEAD_DIM), lambda b, t: (b, t, 0)),
        ],
        out_shape=[
            jax.ShapeDtypeStruct((batch, seq, A_WIDTH), BF16),
            jax.ShapeDtypeStruct((batch, seq, HEAD_DIM), F32),
        ],
        scratch_shapes=[
            pltpu.VMEM((dil, n + 2 * N_SIDE, COL_TILE), BF16),
            pltpu.VMEM((dil, n + 2 * N_SIDE, COL_TILE), BF16),
            pltpu.VMEM((A_HEADS_PER_GROUP, SUPER_TILE, HEAD_DIM), F32),
        ],
        compiler_params=pltpu.CompilerParams(
            dimension_semantics=("arbitrary", "arbitrary"), vmem_limit_bytes=VMEM_LIMIT_BYTES),
        name=f"dilated_g{group}",
    )(h3, h3, h3, h6, h6, h6, h6, bias)
    return o.reshape(batch * seq, A_WIDTH), lse.reshape(batch * seq, HEAD_DIM)


def _post_kernel(x_ref, o0_ref, o1_ref, o2_ref, l0_ref, l1_ref, l2_ref, ag_ref, yb_ref,
                 ga0_ref, ga1_ref, gb0_ref, gb1_ref, wpa_ref, wpb_ref, wo_ref, lng_ref, lnb_ref, y_ref):
    lanes_per_head = HEAD_DIM // A_HEADS_PER_GROUP
    for rc in range(x_ref.shape[0] // POST_CHUNK):
        rows = slice(rc * POST_CHUNK, (rc + 1) * POST_CHUNK)
        l0, l1, l2 = l0_ref[rows, :], l1_ref[rows, :], l2_ref[rows, :]
        top = jnp.maximum(jnp.maximum(l0, l1), l2)
        e0, e1, e2 = jnp.exp(l0 - top), jnp.exp(l1 - top), jnp.exp(l2 - top)
        inv = 1.0 / (e0 + e1 + e2)
        ya_heads = []
        for hd in range(A_HEADS_PER_GROUP):
            sl = slice(hd * HEAD_DIM, (hd + 1) * HEAD_DIM)
            pick = slice(hd * lanes_per_head, hd * lanes_per_head + 1)
            w0 = jnp.broadcast_to((e0 * inv)[:, pick], (POST_CHUNK, HEAD_DIM))
            w1 = jnp.broadcast_to((e1 * inv)[:, pick], (POST_CHUNK, HEAD_DIM))
            w2 = jnp.broadcast_to((e2 * inv)[:, pick], (POST_CHUNK, HEAD_DIM))
            mix = (w0 * o0_ref[rows, sl].astype(F32) + w1 * o1_ref[rows, sl].astype(F32)
                   + w2 * o2_ref[rows, sl].astype(F32))
            ya_heads.append((mix * ag_ref[rows, sl].astype(F32)).astype(BF16))
        ya = jnp.concatenate(ya_heads, axis=-1)
        pa = jnp.dot(ya, wpa_ref[...], preferred_element_type=F32)
        pb = jnp.dot(yb_ref[rows, :], wpb_ref[...], preferred_element_type=F32)
        gate_a = jnp.concatenate([ga0_ref[rows, :], ga1_ref[rows, :]], axis=-1).astype(F32)
        gate_b = jnp.concatenate([gb0_ref[rows, :], gb1_ref[rows, :]], axis=-1).astype(F32)
        merged = (gate_a * pa + gate_b * pb).astype(BF16)
        z = ALPHA * x_ref[rows, :] + jnp.dot(merged, wo_ref[...], preferred_element_type=F32)
        mu = jnp.mean(z, axis=-1, keepdims=True)
        zc = z - mu
        var = jnp.mean(zc * zc, axis=-1, keepdims=True)
        y_ref[rows, :] = zc * lax.rsqrt(var + LN_EPS) * lng_ref[...] + lnb_ref[...]


def _post(x, outs, lses, h, yb, wpa, wpb, wo, ln_g, ln_b, layer):
    tokens = x.shape[0]
    tm = POST_ROWS
    row = lambda width, col=0: pl.BlockSpec((tm, width), lambda i, col=col: (i, col))
    full = lambda a: pl.BlockSpec((None,) + a.shape[1:], lambda i: (layer, 0, 0), pipeline_mode=pl.Buffered(1))
    return pl.pallas_call(
        _post_kernel,
        grid=(tokens // tm,),
        in_specs=[
            row(D_MODEL),
            row(A_WIDTH), row(A_WIDTH), row(A_WIDTH),
            row(HEAD_DIM), row(HEAD_DIM), row(HEAD_DIM),
            row(COL_TILE, T_AG),
            row(B_HEADS * HEAD_DIM),
            row(COL_TILE, T_GATE_A), row(COL_TILE, T_GATE_A + 1),
            row(COL_TILE, T_GATE_B), row(COL_TILE, T_GATE_B + 1),
            full(wpa), full(wpb), full(wo), full(ln_g), full(ln_b),
        ],
        out_specs=row(D_MODEL),
        out_shape=jax.ShapeDtypeStruct((tokens, D_MODEL), F32),
        compiler_params=pltpu.CompilerParams(
            dimension_semantics=("arbitrary",), vmem_limit_bytes=VMEM_LIMIT_BYTES),
        name="post",
    )(x, *outs, *lses, h, yb, h, h, h, h, wpa, wpb, wo, ln_g, ln_b)


def _rope_tables(seq):
    rows = seq // GRID_W
    axis_dim = HEAD_DIM // 2
    inv_freq = ROPE_THETA ** (-jnp.arange(0, axis_dim, 2, dtype=F32) / axis_dim)
    ang_r = jnp.arange(rows).astype(F32)[:, None] * inv_freq[None]
    ang_c = jnp.arange(GRID_W).astype(F32)[:, None] * inv_freq[None]

    def over_positions(per_row, per_col):
        r = jnp.broadcast_to(per_row[:, None, :], (rows, GRID_W, axis_dim // 2))
        c = jnp.broadcast_to(per_col[None, :, :], (rows, GRID_W, axis_dim // 2))
        return r, c

    cos_r, cos_c = over_positions(jnp.cos(ang_r), jnp.cos(ang_c))
    sin_r, sin_c = over_positions(jnp.sin(ang_r), jnp.sin(ang_c))
    cos = jnp.concatenate([cos_r, cos_c, cos_r, cos_c], axis=-1).reshape(seq, HEAD_DIM)
    sin = jnp.concatenate([-sin_r, -sin_c, sin_r, sin_c], axis=-1).reshape(seq, HEAD_DIM)
    return cos, sin


def _pair_rotary_halves(a, first_col, n_heads):
    seg = a[..., first_col:first_col + n_heads * HEAD_DIM]
    seg = seg.reshape(*seg.shape[:-1], n_heads, 2, 2, HEAD_DIM // 4)
    seg = jnp.swapaxes(seg, -3, -2).reshape(*a.shape[:-1], n_heads * HEAD_DIM)
    return jnp.concatenate([a[..., :first_col], seg, a[..., first_col + n_heads * HEAD_DIM:]], axis=-1)


def _alibi_bias(group, dil):
    slopes = 2.0 ** (-ALIBI_MAX_BIAS * np.arange(1, A_HEADS + 1) / A_HEADS)
    slopes = slopes[group * A_HEADS_PER_GROUP:(group + 1) * A_HEADS_PER_GROUP]
    rel = np.arange(DIL_SUB + 2 * N_SIDE)[None, :] - N_SIDE - np.arange(DIL_SUB)[:, None]
    inside = np.abs(rel) <= N_SIDE
    bias = -slopes[:, None, None] * (dil * np.abs(rel)).astype(np.float32)[None]
    return jnp.asarray(np.where(inside[None], bias, MASK_VALUE), dtype=F32)


def _trunk(x, params, biases, cos, sin):
    batch, seq, _ = x.shape
    xf = x.reshape(batch * seq, D_MODEL)
    w_in, b_in, w_rot, b_rot, q_gain, k_gain, wpa, wpb, wo, ln_g, ln_b = params
    for layer in range(DEPTH):
        h = _in_proj(xf, w_in, b_in, w_rot, b_rot, q_gain, k_gain, cos, sin, seq, layer)
        h3 = h.reshape(batch, seq, IN_WIDTH)
        yb = _gqa_flash(h3).reshape(batch * seq, B_HEADS * HEAD_DIM)
        outs, lses = [], []
        for group, (_, dil) in enumerate(A_GROUPS):
            o, lse = _dilated_group(h3, biases[group], group, dil)
            outs.append(o)
            lses.append(lse)
        xf = _post(xf, outs, lses, h, yb, wpa, wpb, wo, ln_g, ln_b, layer)
    return xf.reshape(batch, seq, D_MODEL)


def _prepare_params(w_in, b_in, q_gain, k_gain, w_proj_a, w_proj_b, w_out, ln_g, ln_b):
    rotary_heads = B_HEADS + B_KV_HEADS
    rotary_cols = slice(T_BQ * COL_TILE, (T_BQ + ROTARY_TILES) * COL_TILE)

    return (
        w_in.astype(BF16),
        b_in.reshape(DEPTH, 1, IN_WIDTH),
        _pair_rotary_halves(w_in[..., rotary_cols], 0, rotary_heads).astype(BF16),
        _pair_rotary_halves(b_in[..., rotary_cols], 0, rotary_heads).reshape(DEPTH, 1, ROTARY_TILES * COL_TILE),
        _pair_rotary_halves(q_gain, 0, 1).reshape(DEPTH, 1, HEAD_DIM),
        _pair_rotary_halves(k_gain, 0, 1).reshape(DEPTH, 1, HEAD_DIM),
        w_proj_a.astype(BF16),
        w_proj_b.astype(BF16),
        w_out.astype(BF16),
        ln_g.reshape(DEPTH, 1, D_MODEL),
        ln_b.reshape(DEPTH, 1, D_MODEL),
    )


def kernel(x_prompt, x_sample, w_in, b_in, q_gain, k_gain, w_proj_a, w_proj_b, w_out, ln_g, ln_b):
    params = _prepare_params(w_in, b_in, q_gain, k_gain, w_proj_a, w_proj_b, w_out, ln_g, ln_b)
    biases = [_alibi_bias(group, dil) for group, (_, dil) in enumerate(A_GROUPS)]
    cos, sin = _rope_tables(max(x_prompt.shape[1], x_sample.shape[1]))
    return (_trunk(x_prompt, params, biases, cos, sin), _trunk(x_sample, params, biases, cos, sin))
```

```python
import functools
import math

import jax
import jax.numpy as jnp
import numpy as np
from jax import lax
from jax.experimental import pallas as pl
from jax.experimental.pallas import tpu as pltpu

F32 = jnp.float32
BF16 = jnp.bfloat16

D_MODEL = 1024
DEPTH = 2
HEAD_DIM = 128
GRID_W = 64
A_GROUPS = ((128, 1), (512, 4), (2048, 16))
A_HEADS_PER_GROUP = 4
A_HEADS = 12
A_WIDTH = 512
ALIBI_MAX_BIAS = 8.0
B_HEADS = 8
B_KV_HEADS = 2
B_GROUP = B_HEADS // B_KV_HEADS
ROPE_THETA = 10000.0
IN_WIDTH = 9728
RMS_EPS = 1e-6
LN_EPS = 1e-5
MASK_VALUE = -1e30
ALPHA = (2 * DEPTH) ** 0.25
QK_SCALE = 1.0 / math.sqrt(HEAD_DIM)
BQ_SCALE = QK_SCALE * math.log2(math.e)
SUBLANES = 8
MXU_WIDTH = 256
LOGIT_BOUND = 60.0

COL_TILE = 512
N_COL_TILES = IN_WIDTH // COL_TILE
T_AQ, T_AK, T_AV, T_AG = 0, 3, 6, 9
T_BQ, T_BKV, T_BG = 10, 12, 13
T_GATE_A, T_GATE_B = 15, 17
ROTARY_TILES = T_BKV + 1 - T_BQ
BK_COL128 = T_BKV * COL_TILE // HEAD_DIM
BV_COL128 = BK_COL128 + B_KV_HEADS

N_SIDE = 64

VMEM_LIMIT_BYTES = 56 * 1024 * 1024

SUPER_TILE = 2048
IN_PROJ_CHUNK = 512
GATHER_STRIDE = 4
FLASH_Q_ROWS_CHOICES = (1024, 512)
FLASH_SCORE_ELEMS = 4 * 1024 * 1024
DIL_SUB = 128
DIL_UNROLL = 8
POST_ROWS = 1024
POST_CHUNK = 512


def _in_proj_kernel(x_ref, w_ref, b_ref, wrot_ref, brot_ref, qg_ref, kg_ref, cos_ref, sin_ref, avg_ref,
                    o_ref, xb_ref, hs_ref, hs2_ref):
    j = pl.program_id(1)
    tm = x_ref.shape[0]

    @pl.when(j == 0)
    def _():
        xb_ref[...] = x_ref[...].astype(BF16)

    lane_tiles = COL_TILE // HEAD_DIM
    n_chunks = tm // IN_PROJ_CHUNK

    def column_tile(epilogue, weights=w_ref, bias=b_ref):
        def run():
            for rc in range(n_chunks):
                rows = slice(rc * IN_PROJ_CHUNK, (rc + 1) * IN_PROJ_CHUNK)
                h = jnp.dot(xb_ref[rows, :], weights[...], preferred_element_type=F32) + bias[...]
                epilogue(rc, rows, h)
        return run

    def head_mean_squares(h):
        hsq = (h * h).astype(BF16)
        return jnp.concatenate(
            [jnp.dot(hsq[:, c0:c0 + MXU_WIDTH], avg_ref[...], preferred_element_type=F32)
             for c0 in range(0, h.shape[1], MXU_WIDTH)], axis=-1)

    def rms_rope(hh, ms, gain, rows):
        y = hh * lax.rsqrt(ms + RMS_EPS) * gain
        return y * cos_ref[rows, :] + pltpu.roll(y, HEAD_DIM // 2, 1) * sin_ref[rows, :]

    a_scale = jnp.where(j < T_AK, QK_SCALE, 1.0)
    a_group = j % len(A_GROUPS)

    def plain(rc, rows, h):
        o_ref[rows, :] = (h * a_scale).astype(BF16)

    def class_major(dil):
        n = tm // dil
        per_chunk = IN_PROJ_CHUNK // dil

        def epilogue(rc, rows, h):
            for ln in range(lane_tiles):
                hs_ref[rc, ln] = h[:, ln * HEAD_DIM:(ln + 1) * HEAD_DIM] * a_scale
            src_ref, stride = hs_ref, dil
            if dil == GATHER_STRIDE * GATHER_STRIDE:
                quarter = IN_PROJ_CHUNK // GATHER_STRIDE
                for ln in range(lane_tiles):
                    for c4 in range(GATHER_STRIDE):
                        hs2_ref[rc, ln, c4 * quarter:(c4 + 1) * quarter, :] = (
                            hs_ref[rc, ln, pl.ds(c4, quarter, stride=GATHER_STRIDE), :])
                src_ref, stride = hs2_ref, GATHER_STRIDE
            for c in range(dil):
                if src_ref is hs_ref:
                    first = c
                else:
                    first = (c % GATHER_STRIDE) * (IN_PROJ_CHUNK // GATHER_STRIDE) + c // GATHER_STRIDE
                dst = slice(c * n + rc * per_chunk, c * n + (rc + 1) * per_chunk)
                for ln in range(lane_tiles):
                    o_ref[dst, ln * HEAD_DIM:(ln + 1) * HEAD_DIM] = (
                        src_ref[rc, ln, pl.ds(first, per_chunk, stride=stride), :].astype(BF16))
        return epilogue

    def silu(rc, rows, h):
        o_ref[rows, :] = (h * (jnp.tanh(h) + 1.0)).astype(BF16)

    def rope_q(rc, rows, h):
        ms = head_mean_squares(h)
        for hd in range(lane_tiles):
            sl = slice(hd * HEAD_DIM, (hd + 1) * HEAD_DIM)
            o_ref[rows, sl] = (rms_rope(h[:, sl], ms[:, sl], qg_ref[...], rows) * BQ_SCALE).astype(BF16)

    def rope_k_and_v(rc, rows, h):
        k_width = B_KV_HEADS * HEAD_DIM
        ms = head_mean_squares(h[:, :k_width])
        for hd in range(B_KV_HEADS):
            sl = slice(hd * HEAD_DIM, (hd + 1) * HEAD_DIM)
            o_ref[rows, sl] = rms_rope(h[:, sl], ms[:, sl], kg_ref[...], rows).astype(BF16)
        o_ref[rows, k_width:] = h[:, k_width:].astype(BF16)

    def gate(rc, rows, h):
        o_ref[rows, :] = (0.5 * jnp.tanh(h) + 0.5).astype(BF16)

    for group, (_, dil) in enumerate(A_GROUPS):
        pl.when((j < T_AG) & (a_group == group))(column_tile(plain if dil == 1 else class_major(dil)))
    pl.when((j == T_AG) | (j == T_BG) | (j == T_BG + 1))(column_tile(silu))
    pl.when((j == T_BQ) | (j == T_BQ + 1))(column_tile(rope_q, wrot_ref, brot_ref))
    pl.when(j == T_BKV)(column_tile(rope_k_and_v, wrot_ref, brot_ref))
    pl.when(j >= T_GATE_A)(column_tile(gate))


def _in_proj(x, w, b, w_rot, b_rot, q_gain, k_gain, cos, sin, seq, layer):
    tokens = x.shape[0]
    tm = SUPER_TILE
    assert seq % tm == 0
    pos_blocks = seq // tm
    row_spec = lambda width: pl.BlockSpec((tm, width), lambda i, j: (i, 0))
    pos_spec = pl.BlockSpec((tm, HEAD_DIM), lambda i, j: (i % pos_blocks, 0))
    vec_spec = pl.BlockSpec((None, 1, HEAD_DIM), lambda i, j: (layer, 0, 0))
    head_of_col = np.arange(MXU_WIDTH) // HEAD_DIM
    head_average = jnp.asarray((head_of_col[:, None] == head_of_col[None, :]) / HEAD_DIM, dtype=BF16)
    return pl.pallas_call(
        _in_proj_kernel,
        grid=(tokens // tm, N_COL_TILES),
        in_specs=[
            row_spec(D_MODEL),
            pl.BlockSpec((None, D_MODEL, COL_TILE), lambda i, j: (layer, 0, j)),
            pl.BlockSpec((None, 1, COL_TILE), lambda i, j: (layer, 0, j)),
            pl.BlockSpec((None, D_MODEL, COL_TILE),
                         lambda i, j: (layer, 0, jnp.clip(j - T_BQ, 0, ROTARY_TILES - 1))),
            pl.BlockSpec((None, 1, COL_TILE), lambda i, j: (layer, 0, jnp.clip(j - T_BQ, 0, ROTARY_TILES - 1))),
            vec_spec, vec_spec, pos_spec, pos_spec,
            pl.BlockSpec((MXU_WIDTH, MXU_WIDTH), lambda i, j: (0, 0)),
        ],
        out_specs=pl.BlockSpec((tm, COL_TILE), lambda i, j: (i, j)),
        out_shape=jax.ShapeDtypeStruct((tokens, IN_WIDTH), BF16),
        scratch_shapes=[
            pltpu.VMEM((tm, D_MODEL), BF16),
            pltpu.VMEM((tm // IN_PROJ_CHUNK, COL_TILE // HEAD_DIM, IN_PROJ_CHUNK, HEAD_DIM), F32),
            pltpu.VMEM((tm // IN_PROJ_CHUNK, COL_TILE // HEAD_DIM, IN_PROJ_CHUNK, HEAD_DIM), F32),
        ],
        compiler_params=pltpu.CompilerParams(
            dimension_semantics=("arbitrary", "arbitrary"), vmem_limit_bytes=VMEM_LIMIT_BYTES),
        name="in_proj",
    )(x, w, b, w_rot, b_rot, q_gain, k_gain, cos, sin, head_average)


def _flash_kernel(q_ref, k_ref, v_ref, g_ref, o_ref, qt_ref, vt_ref, m_ref, l_ref, acc_ref, k2max_ref,
                  *, tq, tk, n_chunks):
    i = pl.program_id(2)

    @pl.when(i == 0)
    def _():
        def fill(c, k2max):
            start = pl.multiple_of(c * tk, tk)
            vt_ref[c] = v_ref[pl.ds(start, tk), :].astype(F32).T.astype(BF16)
            kf = k_ref[pl.ds(start, tk), :].astype(F32)
            k2 = jnp.max(jnp.sum(kf * kf, axis=-1, keepdims=True), axis=0, keepdims=True)
            return jnp.maximum(k2max, k2)

        k2max_ref[...] = lax.fori_loop(0, n_chunks, fill, jnp.zeros((1, 1), F32))

    q2 = jnp.zeros((1, 1), F32)
    for hd in range(B_GROUP):
        qt = q_ref[:, hd * HEAD_DIM:(hd + 1) * HEAD_DIM].T
        qt_ref[:, hd * tq:(hd + 1) * tq] = qt
        qf = qt.astype(F32)
        q2 = jnp.maximum(q2, jnp.max(jnp.sum(qf * qf, axis=0, keepdims=True), axis=1, keepdims=True))
    acc_ref[...] = jnp.zeros(acc_ref.shape, F32)
    l_ref[...] = jnp.zeros(l_ref.shape, F32)

    logits_bounded = (q2 * k2max_ref[...])[0, 0] <= LOGIT_BOUND * LOGIT_BOUND

    def scores(c):
        start = pl.multiple_of(c * tk, tk)
        return jnp.dot(k_ref[pl.ds(start, tk), :], qt_ref[...], preferred_element_type=F32)

    def sublane_partial_sums(p):
        return jnp.sum(p.reshape(tk // SUBLANES, SUBLANES, p.shape[1]), axis=0)

    def bounded_body(c, carry):
        p = jnp.exp2(scores(c))
        l_ref[...] += sublane_partial_sums(p)
        acc_ref[...] += jnp.dot(vt_ref[c], p.astype(BF16), preferred_element_type=F32)
        return carry

    def online_body(c, carry):
        st = scores(c)
        m_prev = m_ref[...]
        m_new = jnp.maximum(m_prev, jnp.max(st, axis=0, keepdims=True))
        alpha = jnp.exp2(m_prev - m_new)
        p = jnp.exp2(st - m_new)
        l_ref[...] = alpha * l_ref[...] + sublane_partial_sums(p)
        acc_ref[...] = alpha * acc_ref[...] + jnp.dot(vt_ref[c], p.astype(BF16), preferred_element_type=F32)
        m_ref[...] = m_new
        return carry

    @pl.when(logits_bounded)
    def _():
        lax.fori_loop(0, n_chunks, bounded_body, 0, unroll=2)

    @pl.when(jnp.logical_not(logits_bounded))
    def _():
        m_ref[...] = jnp.full(m_ref.shape, MASK_VALUE, F32)
        lax.fori_loop(0, n_chunks, online_body, 0, unroll=2)

    ot = acc_ref[...] / jnp.sum(l_ref[...], axis=0, keepdims=True)
    for hd in range(B_GROUP):
        sl = slice(hd * HEAD_DIM, (hd + 1) * HEAD_DIM)
        o_ref[:, sl] = (ot[:, hd * tq:(hd + 1) * tq].T * g_ref[:, sl].astype(F32)).astype(BF16)


def _flash_tiles(seq):
    for tq in FLASH_Q_ROWS_CHOICES:
        tk = FLASH_SCORE_ELEMS // (B_GROUP * tq)
        kv_bytes = 2 * 2 * seq * HEAD_DIM * 2 + seq * HEAD_DIM * 2
        block_bytes = 3 * 2 * tq * COL_TILE * 2
        state_bytes = B_GROUP * tq * (HEAD_DIM * (2 + 4) + SUBLANES * 4 + 4)
        score_bytes = FLASH_SCORE_ELEMS * (4 + 2)
        if seq % tq == 0 and kv_bytes + block_bytes + state_bytes + score_bytes <= VMEM_LIMIT_BYTES:
            return tq, tk
    raise ValueError(f"no flash tiling fits VMEM for sequence length {seq}")


def _gqa_flash(h3):
    batch, seq, _ = h3.shape
    tq, tk = _flash_tiles(seq)
    n_chunks = seq // tk
    kernel = functools.partial(_flash_kernel, tq=tq, tk=tk, n_chunks=n_chunks)
    return pl.pallas_call(
        kernel,
        grid=(batch, B_KV_HEADS, seq // tq),
        in_specs=[
            pl.BlockSpec((None, tq, COL_TILE), lambda b, g, i: (b, i, T_BQ + g)),
            pl.BlockSpec((None, seq, HEAD_DIM), lambda b, g, i: (b, 0, BK_COL128 + g)),
            pl.BlockSpec((None, seq, HEAD_DIM), lambda b, g, i: (b, 0, BV_COL128 + g)),
            pl.BlockSpec((None, tq, COL_TILE), lambda b, g, i: (b, i, T_BG + g)),
        ],
        out_specs=pl.BlockSpec((None, tq, COL_TILE), lambda b, g, i: (b, i, g)),
        out_shape=jax.ShapeDtypeStruct((batch, seq, B_HEADS * HEAD_DIM), BF16),
        scratch_shapes=[
            pltpu.VMEM((HEAD_DIM, B_GROUP * tq), BF16),
            pltpu.VMEM((n_chunks, HEAD_DIM, tk), BF16),
            pltpu.VMEM((1, B_GROUP * tq), F32),
            pltpu.VMEM((SUBLANES, B_GROUP * tq), F32),
            pltpu.VMEM((HEAD_DIM, B_GROUP * tq), F32),
            pltpu.VMEM((1, 1), F32),
        ],
        compiler_params=pltpu.CompilerParams(
            dimension_semantics=("arbitrary", "arbitrary", "arbitrary"), vmem_limit_bytes=VMEM_LIMIT_BYTES),
        name="gqa_flash",
    )(h3, h3, h3, h3)


def _dilated_kernel(q_ref, k_ref, v_ref, kp_ref, kn_ref, vp_ref, vn_ref, bias_ref, o_ref, lse_ref,
                    kw_ref, vw_ref, stage_ref, *, dil, length):
    n = SUPER_TILE // dil
    win = DIL_SUB + 2 * N_SIDE
    first_pos = pl.program_id(1) * n
    lanes_per_head = HEAD_DIM // A_HEADS_PER_GROUP
    whole_class_window = n == DIL_SUB

    def sub_block(c, a):
        r0 = pl.multiple_of(a * DIL_SUB, DIL_SUB)
        key_pos = first_pos + a * DIL_SUB - N_SIDE + lax.broadcasted_iota(jnp.int32, (1, win), 1)
        key_ok = (key_pos >= 0) & (key_pos < length)
        token_rows = pl.ds(r0, DIL_SUB) if dil == 1 else pl.ds(c + dil * r0, DIL_SUB, stride=dil)
        lse_cols = []
        for hd in range(A_HEADS_PER_GROUP):
            sl = slice(hd * HEAD_DIM, (hd + 1) * HEAD_DIM)
            q = q_ref[pl.ds(pl.multiple_of(c * n + r0, DIL_SUB), DIL_SUB), sl]
            if whole_class_window:
                rows = pl.ds(pl.multiple_of(c * n, n), n)
                k = jnp.concatenate([kp_ref[c, :, sl], k_ref[rows, sl], kn_ref[c, :, sl]], axis=0)
                v = jnp.concatenate([vp_ref[c, :, sl], v_ref[rows, sl], vn_ref[c, :, sl]], axis=0)
            else:
                k = kw_ref[c, pl.ds(r0, win), sl]
                v = vw_ref[c, pl.ds(r0, win), sl]
            s = lax.dot_general(q, k, (((1,), (1,)), ((), ())), preferred_element_type=F32) + bias_ref[hd]
            s = jnp.where(key_ok, s, MASK_VALUE)
            m = jnp.max(s, axis=-1, keepdims=True)
            p = jnp.exp(s - m)
            l = jnp.sum(p, axis=-1, keepdims=True)
            o = jnp.dot(p.astype(BF16), v, preferred_element_type=F32) / l
            if dil == 1:
                o_ref[token_rows, sl] = o.astype(BF16)
            else:
                stage_ref[hd, token_rows, :] = o
            lse_cols.append(jnp.broadcast_to(m + jnp.log(l), (DIL_SUB, lanes_per_head)))
        lse_ref[token_rows, :] = jnp.concatenate(lse_cols, axis=-1)

    for c in range(0 if whole_class_window else dil):
        rows = slice(c * n, (c + 1) * n)
        kw_ref[c, 0:N_SIDE, :] = kp_ref[c]
        kw_ref[c, N_SIDE:N_SIDE + n, :] = k_ref[rows, :]
        kw_ref[c, N_SIDE + n:, :] = kn_ref[c]
        vw_ref[c, 0:N_SIDE, :] = vp_ref[c]
        vw_ref[c, N_SIDE:N_SIDE + n, :] = v_ref[rows, :]
        vw_ref[c, N_SIDE + n:, :] = vn_ref[c]

    subs_per_class = n // DIL_SUB

    def one_sub_block(sb, carry):
        sub_block(sb // subs_per_class, sb % subs_per_class)
        return carry

    lax.fori_loop(0, SUPER_TILE // DIL_SUB, one_sub_block, 0, unroll=DIL_UNROLL)

    if dil > 1:
        for hd in range(A_HEADS_PER_GROUP):
            o_ref[:, hd * HEAD_DIM:(hd + 1) * HEAD_DIM] = stage_ref[hd].astype(BF16)


def _dilated_group(h3, bias, group, dil):
    batch, seq, _ = h3.shape
    tiles = seq // SUPER_TILE
    n = SUPER_TILE // dil
    halos = n // N_SIDE
    h6 = h3.reshape(batch, tiles, dil, halos, N_SIDE, IN_WIDTH)

    def cur(tile):
        return pl.BlockSpec((None, SUPER_TILE, COL_TILE), lambda b, t: (b, t, tile + group))

    def prev(tile):
        return pl.BlockSpec((None, None, dil, None, N_SIDE, COL_TILE),
                            lambda b, t: (b, jnp.maximum(t - 1, 0), 0, halos - 1, 0, tile + group))

    def nxt(tile):
        return pl.BlockSpec((None, None, dil, None, N_SIDE, COL_TILE),
                            lambda b, t: (b, jnp.minimum(t + 1, tiles - 1), 0, 0, 0, tile + group))

    kernel = functools.partial(_dilated_kernel, dil=dil, length=seq // dil)
    o, lse = pl.pallas_call(
        kernel,
        grid=(batch, tiles),
        in_specs=[
            cur(T_AQ), cur(T_AK), cur(T_AV), prev(T_AK), nxt(T_AK), prev(T_AV), nxt(T_AV),
            pl.BlockSpec((A_HEADS_PER_GROUP, DIL_SUB, DIL_SUB + 2 * N_SIDE), lambda b, t: (0, 0, 0)),
        ],
        out_specs=[
            pl.BlockSpec((None, SUPER_TILE, A_WIDTH), lambda b, t: (b, t, 0)),
            pl.BlockSpec((None, SUPER_TILE, HEAD_DIM), lambda b, t: (b, t, 0)),
        ],
        out_shape=[
            jax.ShapeDtypeStruct((batch, seq, A_WIDTH), BF16),
            jax.ShapeDtypeStruct((batch, seq, HEAD_DIM), F32),
        ],
        scratch_shapes=[
            pltpu.VMEM((dil, n + 2 * N_SIDE, COL_TILE), BF16),
            pltpu.VMEM((dil, n + 2 * N_SIDE, COL_TILE), BF16),
            pltpu.VMEM((A_HEADS_PER_GROUP, SUPER_TILE, HEAD_DIM), F32),
        ],
        compiler_params=pltpu.CompilerParams(
            dimension_semantics=("arbitrary", "arbitrary"), vmem_limit_bytes=VMEM_LIMIT_BYTES),
        name=f"dilated_g{group}",
    )(h3, h3, h3, h6, h6, h6, h6, bias)
    return o.reshape(batch * seq, A_WIDTH), lse.reshape(batch * seq, HEAD_DIM)


def _post_kernel(x_ref, o0_ref, o1_ref, o2_ref, l0_ref, l1_ref, l2_ref, ag_ref, yb_ref,
                 ga0_ref, ga1_ref, gb0_ref, gb1_ref, wpa_ref, wpb_ref, wo_ref, lng_ref, lnb_ref, y_ref):
    lanes_per_head = HEAD_DIM // A_HEADS_PER_GROUP
    for rc in range(x_ref.shape[0] // POST_CHUNK):
        rows = slice(rc * POST_CHUNK, (rc + 1) * POST_CHUNK)
        l0, l1, l2 = l0_ref[rows, :], l1_ref[rows, :], l2_ref[rows, :]
        top = jnp.maximum(jnp.maximum(l0, l1), l2)
        e0, e1, e2 = jnp.exp(l0 - top), jnp.exp(l1 - top), jnp.exp(l2 - top)
        inv = 1.0 / (e0 + e1 + e2)
        ya_heads = []
        for hd in range(A_HEADS_PER_GROUP):
            sl = slice(hd * HEAD_DIM, (hd + 1) * HEAD_DIM)
            pick = slice(hd * lanes_per_head, hd * lanes_per_head + 1)
            w0 = jnp.broadcast_to((e0 * inv)[:, pick], (POST_CHUNK, HEAD_DIM))
            w1 = jnp.broadcast_to((e1 * inv)[:, pick], (POST_CHUNK, HEAD_DIM))
            w2 = jnp.broadcast_to((e2 * inv)[:, pick], (POST_CHUNK, HEAD_DIM))
            mix = (w0 * o0_ref[rows, sl].astype(F32) + w1 * o1_ref[rows, sl].astype(F32)
                   + w2 * o2_ref[rows, sl].astype(F32))
            ya_heads.append((mix * ag_ref[rows, sl].astype(F32)).astype(BF16))
        ya = jnp.concatenate(ya_heads, axis=-1)
        pa = jnp.dot(ya, wpa_ref[...], preferred_element_type=F32)
        pb = jnp.dot(yb_ref[rows, :], wpb_ref[...], preferred_element_type=F32)
        gate_a = jnp.concatenate([ga0_ref[rows, :], ga1_ref[rows, :]], axis=-1).astype(F32)
        gate_b = jnp.concatenate([gb0_ref[rows, :], gb1_ref[rows, :]], axis=-1).astype(F32)
        merged = (gate_a * pa + gate_b * pb).astype(BF16)
        z = ALPHA * x_ref[rows, :] + jnp.dot(merged, wo_ref[...], preferred_element_type=F32)
        mu = jnp.mean(z, axis=-1, keepdims=True)
        zc = z - mu
        var = jnp.mean(zc * zc, axis=-1, keepdims=True)
        y_ref[rows, :] = zc * lax.rsqrt(var + LN_EPS) * lng_ref[...] + lnb_ref[...]


def _post(x, outs, lses, h, yb, wpa, wpb, wo, ln_g, ln_b, layer):
    tokens = x.shape[0]
    tm = POST_ROWS
    row = lambda width, col=0: pl.BlockSpec((tm, width), lambda i, col=col: (i, col))
    full = lambda a: pl.BlockSpec((None,) + a.shape[1:], lambda i: (layer, 0, 0), pipeline_mode=pl.Buffered(1))
    return pl.pallas_call(
        _post_kernel,
        grid=(tokens // tm,),
        in_specs=[
            row(D_MODEL),
            row(A_WIDTH), row(A_WIDTH), row(A_WIDTH),
            row(HEAD_DIM), row(HEAD_DIM), row(HEAD_DIM),
            row(COL_TILE, T_AG),
            row(B_HEADS * HEAD_DIM),
            row(COL_TILE, T_GATE_A), row(COL_TILE, T_GATE_A + 1),
            row(COL_TILE, T_GATE_B), row(COL_TILE, T_GATE_B + 1),
            full(wpa), full(wpb), full(wo), full(ln_g), full(ln_b),
        ],
        out_specs=row(D_MODEL),
        out_shape=jax.ShapeDtypeStruct((tokens, D_MODEL), F32),
        compiler_params=pltpu.CompilerParams(
            dimension_semantics=("arbitrary",), vmem_limit_bytes=VMEM_LIMIT_BYTES),
        name="post",
    )(x, *outs, *lses, h, yb, h, h, h, h, wpa, wpb, wo, ln_g, ln_b)


def _rope_tables(seq):
    rows = seq // GRID_W
    axis_dim = HEAD_DIM // 2
    inv_freq = ROPE_THETA ** (-jnp.arange(0, axis_dim, 2, dtype=F32) / axis_dim)
    ang_r = jnp.arange(rows).astype(F32)[:, None] * inv_freq[None]
    ang_c = jnp.arange(GRID_W).astype(F32)[:, None] * inv_freq[None]

    def over_positions(per_row, per_col):
        r = jnp.broadcast_to(per_row[:, None, :], (rows, GRID_W, axis_dim // 2))
        c = jnp.broadcast_to(per_col[None, :, :], (rows, GRID_W, axis_dim // 2))
        return r, c

    cos_r, cos_c = over_positions(jnp.cos(ang_r), jnp.cos(ang_c))
    sin_r, sin_c = over_positions(jnp.sin(ang_r), jnp.sin(ang_c))
    cos = jnp.concatenate([cos_r, cos_c, cos_r, cos_c], axis=-1).reshape(seq, HEAD_DIM)
    sin = jnp.concatenate([-sin_r, -sin_c, sin_r, sin_c], axis=-1).reshape(seq, HEAD_DIM)
    return cos, sin


def _pair_rotary_halves(a, first_col, n_heads):
    seg = a[..., first_col:first_col + n_heads * HEAD_DIM]
    seg = seg.reshape(*seg.shape[:-1], n_heads, 2, 2, HEAD_DIM // 4)
    seg = jnp.swapaxes(seg, -3, -2).reshape(*a.shape[:-1], n_heads * HEAD_DIM)
    return jnp.concatenate([a[..., :first_col], seg, a[..., first_col + n_heads * HEAD_DIM:]], axis=-1)


def _alibi_bias(group, dil):
    slopes = 2.0 ** (-ALIBI_MAX_BIAS * np.arange(1, A_HEADS + 1) / A_HEADS)
    slopes = slopes[group * A_HEADS_PER_GROUP:(group + 1) * A_HEADS_PER_GROUP]
    rel = np.arange(DIL_SUB + 2 * N_SIDE)[None, :] - N_SIDE - np.arange(DIL_SUB)[:, None]
    inside = np.abs(rel) <= N_SIDE
    bias = -slopes[:, None, None] * (dil * np.abs(rel)).astype(np.float32)[None]
    return jnp.asarray(np.where(inside[None], bias, MASK_VALUE), dtype=F32)


def _trunk(x, params, biases, cos, sin):
    batch, seq, _ = x.shape
    xf = x.reshape(batch * seq, D_MODEL)
    w_in, b_in, w_rot, b_rot, q_gain, k_gain, wpa, wpb, wo, ln_g, ln_b = params
    for layer in range(DEPTH):
        h = _in_proj(xf, w_in, b_in, w_rot, b_rot, q_gain, k_gain, cos, sin, seq, layer)
        h3 = h.reshape(batch, seq, IN_WIDTH)
        yb = _gqa_flash(h3).reshape(batch * seq, B_HEADS * HEAD_DIM)
        outs, lses = [], []
        for group, (_, dil) in enumerate(A_GROUPS):
            o, lse = _dilated_group(h3, biases[group], group, dil)
            outs.append(o)
            lses.append(lse)
        xf = _post(xf, outs, lses, h, yb, wpa, wpb, wo, ln_g, ln_b, layer)
    return xf.reshape(batch, seq, D_MODEL)


def _prepare_params(w_in, b_in, q_gain, k_gain, w_proj_a, w_proj_b, w_out, ln_g, ln_b):
    rotary_heads = B_HEADS + B_KV_HEADS
    rotary_cols = slice(T_BQ * COL_TILE, (T_BQ + ROTARY_TILES) * COL_TILE)

    tile_of_col = np.arange(IN_WIDTH) // COL_TILE
    halved = (tile_of_col == T_AG) | (tile_of_col == T_BG) | (tile_of_col == T_BG + 1) | (tile_of_col >= T_GATE_A)
    col_scale = jnp.asarray(np.where(halved, 0.5, 1.0), dtype=F32)
    return (
        (w_in * col_scale).astype(BF16),
        (b_in * col_scale).reshape(DEPTH, 1, IN_WIDTH),
        _pair_rotary_halves(w_in[..., rotary_cols], 0, rotary_heads).astype(BF16),
        _pair_rotary_halves(b_in[..., rotary_cols], 0, rotary_heads).reshape(DEPTH, 1, ROTARY_TILES * COL_TILE),
        _pair_rotary_halves(q_gain, 0, 1).reshape(DEPTH, 1, HEAD_DIM),
        _pair_rotary_halves(k_gain, 0, 1).reshape(DEPTH, 1, HEAD_DIM),
        w_proj_a.astype(BF16),
        w_proj_b.astype(BF16),
        w_out.astype(BF16),
        ln_g.reshape(DEPTH, 1, D_MODEL),
        ln_b.reshape(DEPTH, 1, D_MODEL),
    )


def kernel(x_prompt, x_sample, w_in, b_in, q_gain, k_gain, w_proj_a, w_proj_b, w_out, ln_g, ln_b):
    params = _prepare_params(w_in, b_in, q_gain, k_gain, w_proj_a, w_proj_b, w_out, ln_g, ln_b)
    biases = [_alibi_bias(group, dil) for group, (_, dil) in enumerate(A_GROUPS)]
    cos, sin = _rope_tables(max(x_prompt.shape[1], x_sample.shape[1]))
    return (_trunk(x_prompt, params, biases, cos, sin), _trunk(x_sample, params, biases, cos, sin))
```
